```python
import jax, jax.numpy as jnp
from jax import lax
import numpy as np


D_MODEL = 2048
BATCH = 1
SEQ = 16384
DEPTH = 1

ATT_HEAD_DIM = 128
N_ATT_HEADS = D_MODEL // 256
N_KV_HEADS = max(1, N_ATT_HEADS // 4)
ATT_ROPE_DIM = ATT_HEAD_DIM // 4
IDX_HEAD_DIM = 64
N_IDX_HEADS = D_MODEL // 128
IDX_ROPE_DIM = IDX_HEAD_DIM // 4
IDX_TOPK = 256
Q_BLOCK = 128
ROPE_THETA = 500000.0
MAX_POS_OFFSET = 1024
HGRN_KDIM = 128
HGRN_VDIM = 128
N_HGRN_HEADS = D_MODEL // 256
HGRN_CHUNK = 64
N_BRANCHES = 2
ATT_WIDTH = N_ATT_HEADS * ATT_HEAD_DIM
HGRN_WIDTH = N_HGRN_HEADS * HGRN_VDIM
N_EXPERTS = 32
MOE_TOPK = 4
D_EXPERT = D_MODEL
SWIGLU_LIMIT = 7.0
SWIGLU_ALPHA = 1.702
MOE_BLOCK = 128
NORM_EPS = 1e-5

IN_WIDTHS = (
    N_ATT_HEADS * ATT_HEAD_DIM,
    N_KV_HEADS * ATT_HEAD_DIM,
    N_KV_HEADS * ATT_HEAD_DIM,
    N_IDX_HEADS * IDX_HEAD_DIM,
    IDX_HEAD_DIM,
    N_IDX_HEADS,
    N_HGRN_HEADS * HGRN_KDIM,
    N_HGRN_HEADS * HGRN_KDIM,
    N_HGRN_HEADS * HGRN_VDIM,
    N_HGRN_HEADS * HGRN_VDIM,
    N_BRANCHES * D_MODEL,
)
D_IN = sum(IN_WIDTHS)

kernel_name = "hybrid_dsa_hgrn2_moe_adaln_block"


def rms_norm(x, g):
    xf = x.astype(jnp.float32)
    y = xf * lax.rsqrt(jnp.mean(xf * xf, axis=-1, keepdims=True) + NORM_EPS)
    return (y * g.astype(jnp.float32)).astype(x.dtype)


def layer_norm(x, g, b):
    xf = x.astype(jnp.float32)
    mu = jnp.mean(xf, axis=-1, keepdims=True)
    var = jnp.mean(jnp.square(xf - mu), axis=-1, keepdims=True)
    y = (xf - mu) * lax.rsqrt(var + NORM_EPS)
    return (y * g.astype(jnp.float32) + b.astype(jnp.float32)).astype(x.dtype)


def partial_rope(x, pos, rot_dim):
    half = rot_dim // 2
    inv_freq = jnp.power(ROPE_THETA, -jnp.arange(half, dtype=jnp.float32) * 2.0 / rot_dim)
    ang = pos.astype(jnp.float32)[..., None] * inv_freq
    cos = jnp.cos(ang)[:, :, None, :].astype(x.dtype)
    sin = jnp.sin(ang)[:, :, None, :].astype(x.dtype)
    x1 = x[..., :half]
    x2 = x[..., half:rot_dim]
    return jnp.concatenate([x1 * cos - x2 * sin, x2 * cos + x1 * sin, x[..., rot_dim:]], axis=-1)


def dsa_attention(q, k, v, q_idx, k_idx, w_idx, topk):
    B, S, H, Dh = q.shape
    G = k.shape[2]
    R = H // G
    nblk = S // Q_BLOCK
    scale = Dh ** -0.5
    key_pos = jnp.arange(S)
    gather = jax.vmap(lambda t, i: t[i])

    def to_blocks(a):
        return a.reshape((B, nblk, Q_BLOCK) + a.shape[2:]).swapaxes(0, 1)

    def block(args):
        qb, qib, wb, start = args
        qpos = start + jnp.arange(Q_BLOCK)
        causal = key_pos[None, :] <= qpos[:, None]
        dots = jnp.einsum('bqhd,bsd->bqhs', qib, k_idx)
        scores = jnp.einsum('bqh,bqhs->bqs', wb, jax.nn.relu(dots)).astype(jnp.float32)
        scores = jnp.where(causal[None], scores, -jnp.inf)
        _, sel = lax.top_k(scores, topk)
        valid = sel <= qpos[None, :, None]
        ks = gather(k, sel)
        vs = gather(v, sel)
        qg = qb.reshape(B, Q_BLOCK, G, R, Dh)
        logits = jnp.einsum('bqgrd,bqkgd->bqgrk', qg, ks).astype(jnp.float32) * scale
        logits = jnp.where(valid[:, :, None, None, :], logits, -jnp.inf)
        p = jax.nn.softmax(logits, axis=-1).astype(v.dtype)
        o = jnp.einsum('bqgrk,bqkgd->bqgrd', p, vs)
        return o.reshape(B, Q_BLOCK, H * Dh)

    starts = jnp.arange(nblk) * Q_BLOCK
    out = lax.map(block, (to_blocks(q), to_blocks(q_idx), to_blocks(w_idx), starts))
    return out.swapaxes(0, 1).reshape(B, S, H * Dh)


def hgrn2(q, f_logit, i, lb):
    B, S, _ = q.shape
    Hh, dk, dv, C = N_HGRN_HEADS, HGRN_KDIM, HGRN_VDIM, HGRN_CHUNK
    nC = S // C
    qf = jax.nn.silu(q.astype(jnp.float32))
    f = lb + (1.0 - lb) * jax.nn.sigmoid(f_logit.astype(jnp.float32))
    logf = jnp.log(f)
    kf = 1.0 - f
    vf = i.astype(jnp.float32)

    def heads(a, d):
        return a.reshape(B, nC, C, Hh, d).transpose(1, 0, 3, 2, 4)

    causal = jnp.tril(jnp.ones((C, C), dtype=bool))

    def step(state, xs):
        qc, kc, vc, gc = xs
        b = jnp.cumsum(gc, axis=2)
        o_inter = jnp.einsum('bhtd,bhde->bhte', qc * jnp.exp(b), state)
        diff = b[:, :, :, None, :] - b[:, :, None, :, :]
        decay = jnp.exp(jnp.where(causal[:, :, None], diff, -jnp.inf))
        A = jnp.einsum('bhtd,bhtsd,bhsd->bhts', qc, decay, kc)
        o = o_inter + jnp.einsum('bhts,bhse->bhte', A, vc)
        b_last = b[:, :, -1:, :]
        state = (jnp.exp(b_last[:, :, 0, :])[..., None] * state
                 + jnp.einsum('bhsd,bhse->bhde', kc * jnp.exp(b_last - b), vc))
        return state, o

    s0 = jnp.zeros((B, Hh, dk, dv), jnp.float32)
    _, o = lax.scan(step, s0, (heads(qf, dk), heads(kf, dk), heads(vf, dv), heads(logf, dk)))
    return o.transpose(1, 0, 3, 2, 4).reshape(B, S, Hh, dv).astype(q.dtype)


def mixer(h, pos, w_in, idx_knorm_g, idx_knorm_b, lb, hgrn_norm_g, w_proj_att, w_proj_hgrn, w_out):
    B, S, D = h.shape
    proj = h @ w_in
    split_pts = np.cumsum(np.array(IN_WIDTHS))[:-1].tolist()
    q, k, v, qi, ki, wi, hq, hf, hi, hg, gates = jnp.split(proj, split_pts, axis=-1)
    q = partial_rope(q.reshape(B, S, N_ATT_HEADS, ATT_HEAD_DIM), pos, ATT_ROPE_DIM)
    k = partial_rope(k.reshape(B, S, N_KV_HEADS, ATT_HEAD_DIM), pos, ATT_ROPE_DIM)
    v = v.reshape(B, S, N_KV_HEADS, ATT_HEAD_DIM)
    qi = partial_rope(qi.reshape(B, S, N_IDX_HEADS, IDX_HEAD_DIM), pos, IDX_ROPE_DIM)
    ki = layer_norm(ki, idx_knorm_g, idx_knorm_b)
    ki = partial_rope(ki[:, :, None, :], pos, IDX_ROPE_DIM)[:, :, 0, :]
    wi = wi * (N_IDX_HEADS ** -0.5 * IDX_HEAD_DIM ** -0.5)
    topk = min(IDX_TOPK, S // 4)
    att = dsa_attention(q, k, v, qi, ki, wi, topk)
    rec = hgrn2(hq, hf, hi, lb)
    rec = rms_norm(rec, hgrn_norm_g) * jax.nn.silu(hg.reshape(B, S, N_HGRN_HEADS, HGRN_VDIM))
    rec = rec.reshape(B, S, HGRN_WIDTH)
    g_att, g_rec = jnp.split(jax.nn.sigmoid(gates), N_BRANCHES, axis=-1)
    merged = g_att * (att @ w_proj_att) + g_rec * (rec @ w_proj_hgrn)
    return merged @ w_out


def moe(h, w_router, b_router, w1, b1, w2, b2):
    B, S, D = h.shape
    xt = h.reshape(-1, D)
    N = xt.shape[0]
    logits = (xt @ w_router).astype(jnp.float32) + b_router.astype(jnp.float32)
    top_v, top_e = lax.top_k(logits, MOE_TOPK)
    gate = jax.nn.softmax(top_v, axis=-1)
    flat_e = top_e.reshape(-1)
    flat_tok = jnp.repeat(jnp.arange(N, dtype=jnp.int32), MOE_TOPK)
    flat_gate = gate.reshape(-1)
    order = jnp.argsort(flat_e)
    se, stok, sg = flat_e[order], flat_tok[order], flat_gate[order]
    counts = jnp.bincount(flat_e, length=N_EXPERTS)
    start = jnp.cumsum(counts) - counts
    pcounts = (counts + MOE_BLOCK - 1) // MOE_BLOCK * MOE_BLOCK
    pend = jnp.cumsum(pcounts)
    pstart = pend - pcounts
    n_assign = N * MOE_TOPK
    dest = pstart[se] + jnp.arange(n_assign, dtype=jnp.int32) - start[se]
    nblk = -(-n_assign // MOE_BLOCK) + N_EXPERTS
    P = nblk * MOE_BLOCK
    buf_tok = jnp.zeros((P,), jnp.int32).at[dest].set(stok)
    blk_e = jnp.clip(jnp.searchsorted(pend, jnp.arange(nblk) * MOE_BLOCK, side='right'), 0, N_EXPERTS - 1)
    xb = xt[buf_tok].reshape(nblk, MOE_BLOCK, D)

    def expert_block(args):
        xe, e = args
        u = xe @ w1[e] + b1[e]
        ug, ul = jnp.split(u, 2, axis=-1)
        ug = jnp.minimum(ug, SWIGLU_LIMIT)
        ul = jnp.clip(ul, -SWIGLU_LIMIT, SWIGLU_LIMIT)
        a = ug * jax.nn.sigmoid(SWIGLU_ALPHA * ug) * (ul + 1.0)
        return a @ w2[e] + b2[e]

    yb = lax.map(expert_block, (xb, blk_e)).reshape(P, D)
    y = yb[dest] * sg[:, None].astype(yb.dtype)
    out = jnp.zeros((N, D), h.dtype).at[stok].add(y)
    return out.reshape(B, S, D)


def setup_inputs(seed: int = 0) -> dict:
    key = jax.random.key(seed)
    ks = jax.random.split(key, 24)
    f32 = jnp.float32

    def nrm(k, shape, scale):
        return jax.random.normal(k, shape, f32) * scale

    D = D_MODEL
    offset = jax.random.randint(ks[2], (BATCH, 1), 0, MAX_POS_OFFSET, dtype=jnp.int32)
    positions = (offset + jnp.arange(SEQ, dtype=jnp.int32)[None, :]).astype(jnp.int32)
    return {
        'x': nrm(ks[0], (BATCH, SEQ, D), 1.0),
        'c': nrm(ks[1], (BATCH, D), 1.0),
        'positions': positions,
        'w_ada': nrm(ks[3], (DEPTH, D, 6 * D), 0.5 * D ** -0.5),
        'b_ada': nrm(ks[4], (DEPTH, 6 * D), 0.02),
        'norm1_g': 1.0 + nrm(ks[5], (DEPTH, D), 0.02),
        'w_in': nrm(ks[6], (DEPTH, D, D_IN), D ** -0.5),
        'idx_knorm_g': 1.0 + nrm(ks[7], (DEPTH, IDX_HEAD_DIM), 0.02),
        'idx_knorm_b': nrm(ks[8], (DEPTH, IDX_HEAD_DIM), 0.02),
        'hgrn_lb_logits': nrm(ks[9], (DEPTH + 1, N_HGRN_HEADS * HGRN_KDIM), 0.5),
        'hgrn_norm_g': 1.0 + nrm(ks[10], (DEPTH, HGRN_VDIM), 0.02),
        'w_proj_att': nrm(ks[11], (DEPTH, ATT_WIDTH, D), ATT_WIDTH ** -0.5),
        'w_proj_hgrn': nrm(ks[12], (DEPTH, HGRN_WIDTH, D), HGRN_WIDTH ** -0.5),
        'w_out': nrm(ks[13], (DEPTH, D, D), D ** -0.5),
        'norm2_g': 1.0 + nrm(ks[14], (DEPTH, D), 0.02),
        'w_router': nrm(ks[15], (DEPTH, D, N_EXPERTS), D ** -0.5),
        'b_router': nrm(ks[16], (DEPTH, N_EXPERTS), 0.01),
        'w1': nrm(ks[17], (DEPTH, N_EXPERTS, D, 2 * D_EXPERT), D ** -0.5),
        'b1': nrm(ks[18], (DEPTH, N_EXPERTS, 2 * D_EXPERT), 0.02),
        'w2': nrm(ks[19], (DEPTH, N_EXPERTS, D_EXPERT, D), D_EXPERT ** -0.5),
        'b2': nrm(ks[20], (DEPTH, N_EXPERTS, D), 0.02),
        'final_norm_g': 1.0 + nrm(ks[21], (D,), 0.02),
    }


def reference(x, c, positions, w_ada, b_ada, norm1_g, w_in, idx_knorm_g, idx_knorm_b,
              hgrn_lb_logits, hgrn_norm_g, w_proj_att, w_proj_hgrn, w_out, norm2_g,
              w_router, b_router, w1, b1, w2, b2, final_norm_g):
    lb_all = jnp.cumsum(jax.nn.softmax(hgrn_lb_logits.astype(jnp.float32), axis=0), axis=0)
    cs = jax.nn.silu(c)
    for l in range(DEPTH):
        mod = cs @ w_ada[l] + b_ada[l]
        sh1, sc1, gt1, sh2, sc2, gt2 = [m[:, None, :] for m in jnp.split(mod, 6, axis=-1)]
        h = rms_norm(x, norm1_g[l]) * (1.0 + sc1) + sh1
        x = x + gt1 * mixer(h, positions, w_in[l], idx_knorm_g[l], idx_knorm_b[l], lb_all[l],
                            hgrn_norm_g[l], w_proj_att[l], w_proj_hgrn[l], w_out[l])
        h = rms_norm(x, norm2_g[l]) * (1.0 + sc2) + sh2
        x = x + gt2 * moe(h, w_router[l], b_router[l], w1[l], b1[l], w2[l], b2[l])
    return rms_norm(x, final_norm_g)
```

```python
import functools

import numpy as np
import jax
import jax.numpy as jnp
from jax import lax
from jax.experimental import pallas as pl
from jax.experimental.pallas import tpu as pltpu

F32 = jnp.float32
BF16 = jnp.bfloat16
I32 = jnp.int32

ATT_HEAD_DIM = 128
N_ATT_HEADS = 8
N_KV_HEADS = 2
ATT_ROPE_DIM = 32
IDX_HEAD_DIM = 64
N_IDX_HEADS = 16
IDX_ROPE_DIM = 16
IDX_TOPK = 256
ROPE_THETA = 500000.0
HGRN_DIM = 128
N_HGRN_HEADS = 8
N_EXPERTS = 32
MOE_TOPK = 4
SWIGLU_LIMIT = 7.0
SWIGLU_ALPHA = 1.702
NORM_EPS = 1e-5

V7X_LANES = 128
INT_MIN = -2147483648
MASKED_LOGIT = -1e30
EXP_CLAMP = 80.0

MIB = 1024 * 1024


def _cparams(sem, vmem_mib):
    return pltpu.CompilerParams(dimension_semantics=sem, vmem_limit_bytes=vmem_mib * MIB)


def _sigmoid(x):
    return 1.0 / (1.0 + jnp.exp(-x))


def _dot(a, b):
    return jnp.dot(a, b, preferred_element_type=F32)


def _dot_nt(a, b):
    return lax.dot_general(a, b, (((1,), (1,)), ((), ())), preferred_element_type=F32)


def _dot_tn(a, b):
    return lax.dot_general(a, b, (((0,), (0,)), ((), ())), preferred_element_type=F32)


def _adaln_kernel(c_ref, w_ref, b_ref, o_ref):
    c = c_ref[...]
    cs = c * _sigmoid(c)
    o_ref[...] = jnp.dot(cs, w_ref[...], preferred_element_type=F32,
                         precision=lax.Precision.HIGHEST) + b_ref[...]


def _adaln(c, w, b):
    d, n = w.shape
    tn = 1024
    c8 = jnp.broadcast_to(c, (8, d))
    out = pl.pallas_call(
        _adaln_kernel,
        grid=(n // tn,),
        in_specs=[pl.BlockSpec((8, d), lambda j: (0, 0)),
                  pl.BlockSpec((d, tn), lambda j: (0, j)),
                  pl.BlockSpec((1, tn), lambda j: (0, j))],
        out_specs=pl.BlockSpec((8, tn), lambda j: (0, j)),
        out_shape=jax.ShapeDtypeStruct((8, n), F32),
        compiler_params=_cparams(("arbitrary",), 40),
        name="adaln",
    )(c8, w, b.reshape(1, n))
    return out[0:1]


def _norm_matmul_kernel(x_ref, g_ref, sc_ref, sh_ref, w_ref, o_ref, h_ref):
    @pl.when(pl.program_id(1) == 0)
    def _():
        x = x_ref[...]
        ms = jnp.mean(x * x, axis=-1, keepdims=True)
        y = x * lax.rsqrt(ms + NORM_EPS) * g_ref[...]
        h_ref[...] = (y * (1.0 + sc_ref[...]) + sh_ref[...]).astype(BF16)

    o_ref[...] = _dot(h_ref[...], w_ref[...]).astype(o_ref.dtype)


def _norm_matmul(x, g, sc, sh, w, out_dtype, tm, tn, name):
    s, d = x.shape
    n = w.shape[1]
    vec = pl.BlockSpec((1, d), lambda i, j: (0, 0))
    return pl.pallas_call(
        _norm_matmul_kernel,
        grid=(s // tm, n // tn),
        in_specs=[pl.BlockSpec((tm, d), lambda i, j: (i, 0)), vec, vec, vec,
                  pl.BlockSpec((d, tn), lambda i, j: (0, j))],
        out_specs=pl.BlockSpec((tm, tn), lambda i, j: (i, j)),
        out_shape=jax.ShapeDtypeStruct((s, n), out_dtype),
        scratch_shapes=[pltpu.VMEM((tm, d), BF16)],
        compiler_params=_cparams(("parallel", "arbitrary"), 56),
        name=name,
    )(x, g, sc, sh, w)


def _rope(x, a, bm, bp, half):
    n = x.shape[-1]
    return x * a + pltpu.roll(x, n - half, 1) * bm + pltpu.roll(x, half, 1) * bp


def _prep_kernel(q_ref, qi_ref, k_ref, kw_ref, ta_ref, ti_ref, kn_ref,
                 qo_ref, ko_ref, qio_ref, kio_ref, wio_ref):
    aa, abm, abp = ta_ref[0], ta_ref[1], ta_ref[2]
    ia, ibm, ibp = ti_ref[0], ti_ref[1], ti_ref[2]
    ah = ATT_ROPE_DIM // 2
    ih = IDX_ROPE_DIM // 2
    scale = ATT_HEAD_DIM ** -0.5
    for h in range(N_ATT_HEADS):
        sl = slice(h * ATT_HEAD_DIM, (h + 1) * ATT_HEAD_DIM)
        qo_ref[:, sl] = (_rope(q_ref[:, sl], aa, abm, abp, ah) * scale).astype(BF16)
    for h in range(N_KV_HEADS):
        sl = slice(h * ATT_HEAD_DIM, (h + 1) * ATT_HEAD_DIM)
        ko_ref[:, sl] = _rope(k_ref[:, sl], aa, abm, abp, ah).astype(BF16)
    for p in range(N_IDX_HEADS // 2):
        sl = slice(p * V7X_LANES, (p + 1) * V7X_LANES)
        y = _rope(qi_ref[:, sl], ia, ibm, ibp, ih).astype(BF16)
        qio_ref[2 * p] = y[:, :IDX_HEAD_DIM]
        qio_ref[2 * p + 1] = y[:, IDX_HEAD_DIM:]
    kw = kw_ref[...]
    lane = lax.broadcasted_iota(I32, kw.shape, 1)
    is_k = lane < IDX_HEAD_DIM
    mu = jnp.sum(jnp.where(is_k, kw, 0.0), axis=-1, keepdims=True) * (1.0 / IDX_HEAD_DIM)
    cen = kw - mu
    var = jnp.sum(jnp.where(is_k, cen * cen, 0.0), axis=-1, keepdims=True) * (1.0 / IDX_HEAD_DIM)
    kn = cen * lax.rsqrt(var + NORM_EPS) * kn_ref[0:1, :] + kn_ref[1:2, :]
    kn = _rope(kn, ia, ibm, ibp, ih)
    kio_ref[...] = kn[:, :IDX_HEAD_DIM].astype(BF16)
    wscale = N_IDX_HEADS ** -0.5 * IDX_HEAD_DIM ** -0.5
    wio_ref[...] = kw[:, IDX_HEAD_DIM:IDX_HEAD_DIM + N_IDX_HEADS] * wscale


def _rope_tables(pos, rot_dim, head_dim):
    half = rot_dim // 2
    inv_freq = jnp.power(ROPE_THETA, -jnp.arange(half, dtype=F32) * 2.0 / rot_dim)
    ang = pos.astype(F32)[:, None] * inv_freq
    cos, sin = jnp.cos(ang), jnp.sin(ang)
    s = pos.shape[0]
    rest = head_dim - rot_dim
    a = jnp.concatenate([cos, cos, jnp.ones((s, rest), F32)], axis=1)
    bm = jnp.concatenate([-sin, jnp.zeros((s, head_dim - half), F32)], axis=1)
    bp = jnp.concatenate([jnp.zeros((s, half), F32), sin, jnp.zeros((s, rest), F32)], axis=1)
    reps = V7X_LANES // head_dim
    return jnp.stack([jnp.tile(t, (1, reps)) for t in (a, bm, bp)])


def _dsa_prep(proj_f, pos, knorm_g, knorm_b, cols, tb):
    s = proj_f.shape[0]
    ta = _rope_tables(pos, ATT_ROPE_DIM, ATT_HEAD_DIM)
    ti = _rope_tables(pos, IDX_ROPE_DIM, IDX_HEAD_DIM)
    kn = jnp.zeros((2, V7X_LANES), F32)
    kn = kn.at[0, :IDX_HEAD_DIM].set(knorm_g).at[1, :IDX_HEAD_DIM].set(knorm_b)
    qw = N_ATT_HEADS * ATT_HEAD_DIM
    qiw = N_IDX_HEADS * IDX_HEAD_DIM
    kw = N_KV_HEADS * ATT_HEAD_DIM
    tab = pl.BlockSpec((3, tb, V7X_LANES), lambda i: (0, i, 0))
    return pl.pallas_call(
        _prep_kernel,
        grid=(s // tb,),
        in_specs=[pl.BlockSpec((tb, qw), lambda i: (i, cols["q"] // qw)),
                  pl.BlockSpec((tb, qiw), lambda i: (i, cols["qi"] // qiw)),
                  pl.BlockSpec((tb, kw), lambda i: (i, cols["k"] // kw)),
                  pl.BlockSpec((tb, V7X_LANES), lambda i: (i, cols["kiwi"] // V7X_LANES)),
                  tab, tab,
                  pl.BlockSpec((2, V7X_LANES), lambda i: (0, 0))],
        out_specs=[pl.BlockSpec((tb, qw), lambda i: (i, 0)),
                   pl.BlockSpec((tb, kw), lambda i: (i, 0)),
                   pl.BlockSpec((N_IDX_HEADS, tb, IDX_HEAD_DIM), lambda i: (0, i, 0)),
                   pl.BlockSpec((tb, IDX_HEAD_DIM), lambda i: (i, 0)),
                   pl.BlockSpec((tb, N_IDX_HEADS), lambda i: (i, 0))],
        out_shape=[jax.ShapeDtypeStruct((s, qw), BF16),
                   jax.ShapeDtypeStruct((s, kw), BF16),
                   jax.ShapeDtypeStruct((N_IDX_HEADS, s, IDX_HEAD_DIM), BF16),
                   jax.ShapeDtypeStruct((s, IDX_HEAD_DIM), BF16),
                   jax.ShapeDtypeStruct((s, N_IDX_HEADS), F32)],
        compiler_params=_cparams(("parallel",), 40),
        name="dsa_prep",
    )(proj_f, proj_f, proj_f, proj_f, ta, ti, kn)


def _order_key(x):
    b = lax.bitcast_convert_type(x, I32)
    return jnp.where(b < 0, b ^ jnp.int32(0x7FFFFFFF), b)


def _lane_chunk_sum(m, tk):
    acc = m[:, 0:V7X_LANES]
    for c in range(1, tk // V7X_LANES):
        acc = acc + m[:, c * V7X_LANES:(c + 1) * V7X_LANES]
    return acc


def _dsa_kernel(qi_ref, wi_ref, q_ref, ki_ref, k_ref, v_ref, o_ref,
                key_ref, acc_ref, m_ref, l_ref, *, tq, tk, topk):
    i = pl.program_id(0)
    n_kt = (i * tq + tq + tk - 1) // tk
    qpos = i * tq + lax.broadcasted_iota(I32, (tq, 1), 0)
    wi = wi_ref[...]

    def score_tile(kt, row_max):
        k0 = pl.multiple_of(kt * tk, tk)
        kit = ki_ref[pl.ds(k0, tk), :]
        acc = jnp.zeros((tq, tk), F32)
        for h in range(N_IDX_HEADS):
            d = _dot_nt(qi_ref[h], kit)
            acc = acc + wi[:, h:h + 1] * jnp.maximum(d, 0.0)
        kpos = k0 + lax.broadcasted_iota(I32, (1, tk), 1)
        key = jnp.where(kpos <= qpos, _order_key(acc), jnp.int32(INT_MIN))
        key_ref[kt] = key
        return jnp.maximum(row_max, jnp.max(key, axis=-1, keepdims=True))

    row_max = lax.fori_loop(0, n_kt, score_tile, jnp.full((tq, 1), INT_MIN, I32))

    def count_ge(thr):
        def body(kt, acc):
            m = jnp.where(key_ref[kt] >= thr, 1, 0).astype(I32)
            return acc + _lane_chunk_sum(m, tk)
        acc = lax.fori_loop(0, n_kt, body, jnp.zeros((tq, V7X_LANES), I32))
        return jnp.sum(acc, axis=-1, keepdims=True)

    def bisect_cond(st):
        lo, hi, _ = st
        return jnp.max(jnp.where(lo + 1 < hi, 1, 0)) > 0

    def bisect_body(st):
        lo, hi, cnt_lo = st
        mid = (lo & hi) + ((lo ^ hi) >> 1)
        cnt = count_ge(mid)
        ge = cnt >= topk
        lo = jnp.where(ge, mid, lo)
        cnt_lo = jnp.where(ge, cnt, cnt_lo)
        hi = jnp.where(ge, jnp.where(cnt == topk, mid + 1, hi), mid)
        return lo, hi, cnt_lo

    lo0 = jnp.full((tq, 1), INT_MIN, I32)
    cnt0 = jnp.full((tq, 1), n_kt * tk, I32)
    thr, _, cnt_thr = lax.while_loop(bisect_cond, bisect_body, (lo0, row_max + 1, cnt0))

    tie_rows = (cnt_thr > topk) & (thr > INT_MIN)

    @pl.when(jnp.max(jnp.where(tie_rows, 1, 0)) > 0)
    def _():
        need = topk - count_ge(thr + 1)
        a = lax.broadcasted_iota(I32, (tk, tk), 0)
        b = lax.broadcasted_iota(I32, (tk, tk), 1)
        before = jnp.where(a < b, 1.0, 0.0).astype(BF16)

        def body(kt, seen):
            key = key_ref[kt]
            eq = (key == thr) & tie_rows
            eqf = jnp.where(eq, 1.0, 0.0)
            rank = seen + _dot(eqf.astype(BF16), before)
            drop = eq & (rank >= need.astype(F32))
            key_ref[kt] = jnp.where(drop, jnp.int32(INT_MIN), key)
            return seen + jnp.sum(eqf, axis=-1, keepdims=True)

        lax.fori_loop(0, n_kt, body, jnp.zeros((tq, 1), F32))

    thr = jnp.maximum(thr, INT_MIN + 1)

    m_ref[...] = jnp.full(m_ref.shape, MASKED_LOGIT, F32)
    l_ref[...] = jnp.zeros(l_ref.shape, F32)
    acc_ref[...] = jnp.zeros(acc_ref.shape, F32)
    rep = N_ATT_HEADS // N_KV_HEADS

    def attend_tile(kt, carry):
        k0 = pl.multiple_of(kt * tk, tk)
        sel = key_ref[kt] >= thr
        for g in range(N_KV_HEADS):
            gs = slice(g * ATT_HEAD_DIM, (g + 1) * ATT_HEAD_DIM)
            kg = k_ref[pl.ds(k0, tk), gs]
            vg = v_ref[pl.ds(k0, tk), gs]
            for r in range(rep):
                h = g * rep + r
                qh = q_ref[:, h * ATT_HEAD_DIM:(h + 1) * ATT_HEAD_DIM]
                s = jnp.where(sel, _dot_nt(qh, kg), MASKED_LOGIT)
                m_old = m_ref[h]
                m_new = jnp.maximum(m_old, jnp.max(s, axis=-1, keepdims=True))
                alpha = jnp.exp(m_old - m_new)
                p = jnp.exp(s - m_new)
                l_ref[h] = alpha * l_ref[h] + jnp.sum(p, axis=-1, keepdims=True)
                acc_ref[h] = alpha * acc_ref[h] + _dot(p.astype(BF16), vg)
                m_ref[h] = m_new
        return carry

    lax.fori_loop(0, n_kt, attend_tile, 0)
    for h in range(N_ATT_HEADS):
        o_ref[:, h * ATT_HEAD_DIM:(h + 1) * ATT_HEAD_DIM] = (acc_ref[h] / l_ref[h]).astype(o_ref.dtype)


def _dsa(qi_hm, wi, q, ki, k, v_src, v_col, tq, tk, topk):
    s = q.shape[0]
    kw = N_KV_HEADS * ATT_HEAD_DIM
    qw = N_ATT_HEADS * ATT_HEAD_DIM
    whole = lambda shape, idx: pl.BlockSpec(shape, idx, pipeline_mode=pl.Buffered(1))
    kern = functools.partial(_dsa_kernel, tq=tq, tk=tk, topk=topk)
    return pl.pallas_call(
        kern,
        grid=(s // tq,),
        in_specs=[pl.BlockSpec((N_IDX_HEADS, tq, IDX_HEAD_DIM), lambda i: (0, i, 0)),
                  pl.BlockSpec((tq, N_IDX_HEADS), lambda i: (i, 0)),
                  pl.BlockSpec((tq, qw), lambda i: (i, 0)),
                  whole((s, IDX_HEAD_DIM), lambda i: (0, 0)),
                  whole((s, kw), lambda i: (0, 0)),
                  whole((s, kw), lambda i: (0, v_col // kw))],
        out_specs=pl.BlockSpec((tq, qw), lambda i: (i, 0)),
        out_shape=jax.ShapeDtypeStruct((s, qw), BF16),
        scratch_shapes=[pltpu.VMEM((s // tk, tq, tk), I32),
                        pltpu.VMEM((N_ATT_HEADS, tq, ATT_HEAD_DIM), F32),
                        pltpu.VMEM((N_ATT_HEADS, tq, 1), F32),
                        pltpu.VMEM((N_ATT_HEADS, tq, 1), F32)],
        compiler_params=_cparams(("parallel",), 56),
        name="dsa",
    )(qi_hm, wi, q, ki, k, v_src)


HGRN_BLOCK = 256
HGRN_LEVELS = (32, 64, 128, 256)


def _hgrn_constants():
    tb = HGRN_BLOCK
    t = np.arange(tb)[:, None]
    u = np.arange(tb)[None, :]
    lower = (u <= t).astype(np.float32)
    mats = [lower]
    c0 = HGRN_LEVELS[0]
    ref0 = c0 * (t // c0) + c0 // 2 - 1
    mats.append(lower - (u <= ref0))
    for c in HGRN_LEVELS[1:-1]:
        ref = c * (t // c) + c // 2 - 1
        mats.append(lower - (u <= ref))
    level = np.full((tb, tb), -1.0, np.float32)
    level[(t // c0 == u // c0) & (u <= t)] = 0.0
    for li, c in enumerate(HGRN_LEVELS[1:], start=1):
        sel = (t // c == u // c) & (t % c >= c // 2) & (u % c < c // 2)
        level[sel] = float(li)
    return np.concatenate(mats, axis=0), level


def _hgrn_kernel(q_ref, f_ref, i_ref, g_ref, lb_ref, gn_ref, mats_ref, lvl_ref, o_ref, st_ref):
    tb = HGRN_BLOCK
    d = HGRN_DIM

    @pl.when(pl.program_id(1) == 0)
    def _():
        st_ref[...] = jnp.zeros(st_ref.shape, F32)

    q = q_ref[...]
    qf = q * _sigmoid(q)
    lb = lb_ref[0]
    f = lb + (1.0 - lb) * _sigmoid(f_ref[...])
    logf = jnp.log(f)
    kf = 1.0 - f
    v = i_ref[...].astype(BF16)

    g1 = logf.astype(BF16)
    r1 = logf - g1.astype(F32)
    g2 = r1.astype(BF16)
    g3 = (r1 - g2.astype(F32)).astype(BF16)
    sums = _dot(mats_ref[...], jnp.concatenate([g1, g2, g3], axis=1))
    sums = sums[:, 0:d] + sums[:, d:2 * d] + sums[:, 2 * d:3 * d]
    n_mid = len(HGRN_LEVELS) - 1
    cum = sums[0:tb]
    rel = [sums[(l + 1) * tb:(l + 2) * tb] for l in range(n_mid)]
    rel.append(cum - cum[tb // 2 - 1:tb // 2, :])

    lvl = lvl_ref[...]
    amat = jnp.zeros((tb, tb), F32)
    for l, e in enumerate(rel):
        cap = EXP_CLAMP if l == 0 else 0.0
        qs = (qf * jnp.exp(jnp.minimum(e, cap))).astype(BF16)
        ks = (kf * jnp.exp(jnp.minimum(-e, cap))).astype(BF16)
        amat = jnp.where(lvl == float(l), _dot_nt(qs, ks), amat)

    st = st_ref[...]
    o = _dot(amat.astype(BF16), v) + _dot_nt((qf * jnp.exp(cum)).astype(BF16), st.astype(BF16))
    last = cum[tb - 1:tb, :]
    ks_state = (kf * jnp.exp(last - cum)).astype(BF16)
    st_ref[...] = st * jnp.exp(last) + _dot_tn(v, ks_state)

    ms = jnp.mean(o * o, axis=-1, keepdims=True)
    y = o * lax.rsqrt(ms + NORM_EPS) * gn_ref[...]
    gate = g_ref[...].astype(F32)
    o_ref[...] = (y * (gate * _sigmoid(gate))).astype(o_ref.dtype)


def _hgrn(proj_f, proj_b, lb, gn, cols_f, cols_b):
    s = proj_f.shape[0]
    tb = HGRN_BLOCK
    d = HGRN_DIM
    mats_np, lvl_np = _hgrn_constants()
    mats = jnp.asarray(mats_np, BF16)
    lvl = jnp.asarray(lvl_np, F32)
    col = lambda off: (lambda h, j: (j, off // d + h))
    const = lambda shape: pl.BlockSpec(shape, lambda h, j: (0,) * len(shape))
    return pl.pallas_call(
        _hgrn_kernel,
        grid=(N_HGRN_HEADS, s // tb),
        in_specs=[pl.BlockSpec((tb, d), col(cols_f["hq"])),
                  pl.BlockSpec((tb, d), col(cols_f["hf"])),
                  pl.BlockSpec((tb, d), col(cols_f["hi"])),
                  pl.BlockSpec((tb, d), col(cols_b["hg"])),
                  pl.BlockSpec((1, 1, d), lambda h, j: (h, 0, 0)),
                  const((1, d)),
                  const(mats_np.shape),
                  const(lvl_np.shape)],
        out_specs=pl.BlockSpec((tb, d), lambda h, j: (j, h)),
        out_shape=jax.ShapeDtypeStruct((s, N_HGRN_HEADS * d), BF16),
        scratch_shapes=[pltpu.VMEM((d, d), F32)],
        compiler_params=_cparams(("parallel", "arbitrary"), 32),
        name="hgrn",
    )(proj_f, proj_f, proj_f, proj_b, lb.reshape(N_HGRN_HEADS, 1, d), gn.reshape(1, d), mats, lvl)


def _post_kernel(att_ref, rec_ref, ga_ref, gr_ref, x_ref, wpa_ref, wph_ref, wo_ref,
                 gt_ref, g2_ref, sc_ref, sh_ref, wr_ref, br_ref,
                 x1_ref, h2_ref, te_ref, tg_ref, sel_ref):
    a = _dot(att_ref[...], wpa_ref[...])
    r = _dot(rec_ref[...], wph_ref[...])
    merged = _sigmoid(ga_ref[...].astype(F32)) * a + _sigmoid(gr_ref[...].astype(F32)) * r
    mix = _dot(merged.astype(BF16), wo_ref[...])
    x1 = x_ref[...] + gt_ref[...] * mix
    x1_ref[...] = x1
    ms = jnp.mean(x1 * x1, axis=-1, keepdims=True)
    h2 = x1 * lax.rsqrt(ms + NORM_EPS) * g2_ref[...]
    h2 = h2 * (1.0 + sc_ref[...]) + sh_ref[...]
    h2_ref[...] = h2
    logits = jnp.dot(h2, wr_ref[...], preferred_element_type=F32,
                     precision=lax.Precision.HIGHEST) + br_ref[...]
    ne = logits.shape[-1]
    lane = lax.broadcasted_iota(I32, logits.shape, 1)
    work = logits
    vals, idxs = [], []
    sel = jnp.zeros(logits.shape, F32)
    for _ in range(MOE_TOPK):
        m = jnp.max(work, axis=-1, keepdims=True)
        idx = jnp.min(jnp.where(work == m, lane, ne), axis=-1, keepdims=True)
        hit = lane == idx
        vals.append(m)
        idxs.append(idx)
        sel = jnp.where(hit, 1.0, sel)
        work = jnp.where(hit, -jnp.inf, work)
    es = [jnp.exp(vk - vals[0]) for vk in vals]
    denom = es[0]
    for e in es[1:]:
        denom = denom + e
    lane4 = lax.broadcasted_iota(I32, te_ref.shape, 1)
    te = jnp.zeros(te_ref.shape, I32)
    tg = jnp.zeros(tg_ref.shape, F32)
    for kk in range(MOE_TOPK):
        te = jnp.where(lane4 == kk, idxs[kk], te)
        tg = jnp.where(lane4 == kk, es[kk] / denom, tg)
    te_ref[...] = te
    tg_ref[...] = tg
    sel_ref[...] = sel.astype(sel_ref.dtype)


def _post_mixer(att, rec, proj_b, gates_col, x, wpa, wph, wo, gt1, g2, sc2, sh2, wr, br, tm):
    s, d = x.shape
    aw = att.shape[1]
    rw = rec.shape[1]
    ne = wr.shape[1]
    const = lambda shape: pl.BlockSpec(shape, lambda i: (0,) * len(shape), pipeline_mode=pl.Buffered(1))
    row = lambda w: pl.BlockSpec((tm, w), lambda i: (i, 0))
    gcol = gates_col // d
    return pl.pallas_call(
        _post_kernel,
        grid=(s // tm,),
        in_specs=[row(aw), row(rw),
                  pl.BlockSpec((tm, d), lambda i: (i, gcol)),
                  pl.BlockSpec((tm, d), lambda i: (i, gcol + 1)),
                  row(d),
                  const((aw, d)), const((rw, d)), const((d, d)),
                  const((1, d)), const((1, d)), const((1, d)), const((1, d)),
                  const((d, ne)), const((1, ne))],
        out_specs=[row(d), row(d), row(MOE_TOPK), row(MOE_TOPK), row(ne)],
        out_shape=[jax.ShapeDtypeStruct((s, d), F32),
                   jax.ShapeDtypeStruct((s, d), F32),
                   jax.ShapeDtypeStruct((s, MOE_TOPK), I32),
                   jax.ShapeDtypeStruct((s, MOE_TOPK), F32),
                   jax.ShapeDtypeStruct((s, ne), BF16)],
        compiler_params=_cparams(("parallel",), 56),
        name="post_mixer",
    )(att, rec, proj_b, proj_b, x, wpa, wph, wo, gt1, g2, sc2, sh2, wr, br)


def _rank_kernel(sel_ref, rank_ref, cnt_ref, carry_ref):
    @pl.when(pl.program_id(0) == 0)
    def _():
        carry_ref[...] = jnp.zeros(carry_ref.shape, F32)

    sel = sel_ref[...]
    tt = sel.shape[0]
    a = lax.broadcasted_iota(I32, (tt, tt), 0)
    b = lax.broadcasted_iota(I32, (tt, tt), 1)
    before = jnp.where(b < a, 1.0, 0.0).astype(BF16)
    carry = carry_ref[...]
    rank_ref[...] = _dot(before, sel) + carry
    carry = carry + jnp.sum(sel.astype(F32), axis=0, keepdims=True)
    carry_ref[...] = carry
    cnt_ref[...] = carry


def _expert_ranks(sel, tt):
    s, ne = sel.shape
    return pl.pallas_call(
        _rank_kernel,
        grid=(s // tt,),
        in_specs=[pl.BlockSpec((tt, ne), lambda i: (i, 0))],
        out_specs=[pl.BlockSpec((tt, ne), lambda i: (i, 0)),
                   pl.BlockSpec((1, ne), lambda i: (0, 0))],
        out_shape=[jax.ShapeDtypeStruct((s, ne), F32),
                   jax.ShapeDtypeStruct((1, ne), F32)],
        scratch_shapes=[pltpu.VMEM((1, ne), F32)],
        compiler_params=_cparams(("arbitrary",), 16),
        name="expert_ranks",
    )(sel)


def _row_copy(src_ref, src_row, dst_ref, dst_row, sem):
    return pltpu.make_async_copy(src_ref.at[pl.ds(src_row, 1)], dst_ref.at[pl.ds(dst_row, 1)], sem)


def _scatter_kernel(dest_ref, h_ref, xb_in_ref, xb_ref, sem, *, tt):
    del xb_in_ref
    base = pl.program_id(0) * tt * MOE_TOPK

    def issue(t, c):
        for kk in range(MOE_TOPK):
            _row_copy(h_ref, t, xb_ref, dest_ref[base + t * MOE_TOPK + kk], sem).start()
        return c

    lax.fori_loop(0, tt, issue, 0)

    def drain(t, c):
        for kk in range(MOE_TOPK):
            _row_copy(h_ref, 0, xb_ref, 0, sem).wait()
        return c

    lax.fori_loop(0, tt, drain, 0)


def _scatter_rows(dest_flat, h2, n_rows, tt):
    s, d = h2.shape
    xb0 = jnp.zeros((n_rows, d), h2.dtype)
    grid_spec = pltpu.PrefetchScalarGridSpec(
        num_scalar_prefetch=1,
        grid=(s // tt,),
        in_specs=[pl.BlockSpec((tt, d), lambda i, dest: (i, 0)),
                  pl.BlockSpec(memory_space=pl.ANY)],
        out_specs=pl.BlockSpec(memory_space=pl.ANY),
        scratch_shapes=[pltpu.SemaphoreType.DMA(())],
    )
    return pl.pallas_call(
        functools.partial(_scatter_kernel, tt=tt),
        grid_spec=grid_spec,
        out_shape=jax.ShapeDtypeStruct((n_rows, d), h2.dtype),
        input_output_aliases={2: 0},
        compiler_params=_cparams(("arbitrary",), 32),
        name="moe_scatter",
    )(dest_flat, h2, xb0)


def _expert_kernel(sbe_ref, sbv_ref, nused_ref, x_ref, w1g_ref, w1l_ref, b1g_ref, b1l_ref,
                   w2_ref, b2_ref, y_ref):
    sb = pl.program_id(0)
    c = pl.program_id(1)

    @pl.when(sb < nused_ref[0])
    def _():
        rows = lax.broadcasted_iota(I32, (x_ref.shape[0], 1), 0)
        x = jnp.where(rows < sbv_ref[sb], x_ref[...], 0.0).astype(BF16)
        ug = _dot(x, w1g_ref[0]) + b1g_ref[0]
        ul = _dot(x, w1l_ref[0]) + b1l_ref[0]
        ug = jnp.minimum(ug, SWIGLU_LIMIT)
        ul = jnp.clip(ul, -SWIGLU_LIMIT, SWIGLU_LIMIT)
        act = ug * _sigmoid(SWIGLU_ALPHA * ug) * (ul + 1.0)
        part = _dot(act.astype(BF16), w2_ref[0])

        @pl.when(c == 0)
        def _():
            y_ref[...] = part + b2_ref[0]

        @pl.when(c > 0)
        def _():
            y_ref[...] += part

    @pl.when((sb >= nused_ref[0]) & (c == 0))
    def _():
        y_ref[...] = jnp.zeros(y_ref.shape, y_ref.dtype)


def _experts(sb_expert, sb_valid, n_used, xb, w1, b1, w2, b2, rb, th):
    n_rows, d = xb.shape
    ne, _, two_de = w1.shape
    de = two_de // 2
    nch = de // th
    n_sb = n_rows // rb
    blk = lambda sb, nu: jnp.minimum(sb, nu[0] - 1)
    exp = lambda sb, e, nu: e[blk(sb, nu)]
    chk = lambda sb, c, nu: jnp.where(sb < nu[0], c, nch - 1)
    grid_spec = pltpu.PrefetchScalarGridSpec(
        num_scalar_prefetch=3,
        grid=(n_sb, nch),
        in_specs=[pl.BlockSpec((rb, d), lambda sb, c, e, v, nu: (blk(sb, nu), 0)),
                  pl.BlockSpec((1, d, th), lambda sb, c, e, v, nu: (exp(sb, e, nu), 0, chk(sb, c, nu))),
                  pl.BlockSpec((1, d, th), lambda sb, c, e, v, nu: (exp(sb, e, nu), 0, nch + chk(sb, c, nu))),
                  pl.BlockSpec((1, 1, th), lambda sb, c, e, v, nu: (exp(sb, e, nu), 0, chk(sb, c, nu))),
                  pl.BlockSpec((1, 1, th), lambda sb, c, e, v, nu: (exp(sb, e, nu), 0, nch + chk(sb, c, nu))),
                  pl.BlockSpec((1, th, d), lambda sb, c, e, v, nu: (exp(sb, e, nu), chk(sb, c, nu), 0)),
                  pl.BlockSpec((1, 1, d), lambda sb, c, e, v, nu: (exp(sb, e, nu), 0, 0))],
        out_specs=pl.BlockSpec((rb, d), lambda sb, c, e, v, nu: (sb, 0)),
    )
    return pl.pallas_call(
        _expert_kernel,
        grid_spec=grid_spec,
        out_shape=jax.ShapeDtypeStruct((n_rows, d), F32),
        compiler_params=_cparams(("arbitrary", "arbitrary"), 56),
        name="moe_experts",
    )(sb_expert, sb_valid, n_used, xb, w1, w1, b1.reshape(ne, 1, two_de), b1.reshape(ne, 1, two_de),
      w2, b2.reshape(ne, 1, d))


def _combine_kernel(dest_ref, yb_ref, x1_ref, tg_ref, gt_ref, gn_ref, o_ref, buf_ref, sem, *, tt):
    base = pl.program_id(0) * tt * MOE_TOPK

    def issue(t, c):
        for kk in range(MOE_TOPK):
            _row_copy(yb_ref, dest_ref[base + t * MOE_TOPK + kk], buf_ref.at[kk], t, sem).start()
        return c

    lax.fori_loop(0, tt, issue, 0)

    def drain(t, c):
        for kk in range(MOE_TOPK):
            _row_copy(yb_ref, 0, buf_ref.at[kk], 0, sem).wait()
        return c

    lax.fori_loop(0, tt, drain, 0)

    tg = tg_ref[...]
    moe = tg[:, 0:1] * buf_ref[0]
    for kk in range(1, MOE_TOPK):
        moe = moe + tg[:, kk:kk + 1] * buf_ref[kk]
    x2 = x1_ref[...] + gt_ref[...] * moe
    ms = jnp.mean(x2 * x2, axis=-1, keepdims=True)
    o_ref[...] = x2 * lax.rsqrt(ms + NORM_EPS) * gn_ref[...]


def _combine(dest_flat, yb, x1, tg, gt2, gn, tt):
    s, d = x1.shape
    grid_spec = pltpu.PrefetchScalarGridSpec(
        num_scalar_prefetch=1,
        grid=(s // tt,),
        in_specs=[pl.BlockSpec(memory_space=pl.ANY),
                  pl.BlockSpec((tt, d), lambda i, dest: (i, 0)),
                  pl.BlockSpec((tt, MOE_TOPK), lambda i, dest: (i, 0)),
                  pl.BlockSpec((1, d), lambda i, dest: (0, 0)),
                  pl.BlockSpec((1, d), lambda i, dest: (0, 0))],
        out_specs=pl.BlockSpec((tt, d), lambda i, dest: (i, 0)),
        scratch_shapes=[pltpu.VMEM((MOE_TOPK, tt, d), F32),
                        pltpu.SemaphoreType.DMA(())],
    )
    return pl.pallas_call(
        functools.partial(_combine_kernel, tt=tt),
        grid_spec=grid_spec,
        out_shape=jax.ShapeDtypeStruct((s, d), F32),
        compiler_params=_cparams(("arbitrary",), 40),
        name="moe_combine",
    )(dest_flat, yb, x1, tg, gt2, gn)


MOE_ROW_BLOCK = 512
MOE_HIDDEN_TILE = 1024


def _pad_cols(w, n):
    return jnp.pad(w, ((0, 0), (0, n - w.shape[1])))


def kernel(x, c, positions, w_ada, b_ada, norm1_g, w_in, idx_knorm_g, idx_knorm_b, hgrn_lb_logits,
           hgrn_norm_g, w_proj_att, w_proj_hgrn, w_out, norm2_g, w_router, b_router, w1, b1, w2, b2,
           final_norm_g):
    bsz, s, d = x.shape
    assert bsz == 1 and w_ada.shape[0] == 1
    xs = x.reshape(s, d)
    pos = positions.reshape(s)

    mod = _adaln(c, w_ada[0], b_ada[0])
    sh1, sc1, gt1, sh2, sc2, gt2 = [mod[:, i * d:(i + 1) * d] for i in range(6)]

    widths = dict(q=N_ATT_HEADS * ATT_HEAD_DIM, k=N_KV_HEADS * ATT_HEAD_DIM, v=N_KV_HEADS * ATT_HEAD_DIM,
                  qi=N_IDX_HEADS * IDX_HEAD_DIM, ki=IDX_HEAD_DIM, wi=N_IDX_HEADS,
                  hq=N_HGRN_HEADS * HGRN_DIM, hf=N_HGRN_HEADS * HGRN_DIM, hi=N_HGRN_HEADS * HGRN_DIM,
                  hg=N_HGRN_HEADS * HGRN_DIM, gates=2 * d)
    src, off = {}, 0
    for name in ("q", "k", "v", "qi", "ki", "wi", "hq", "hf", "hi", "hg", "gates"):
        src[name] = (off, widths[name])
        off += widths[name]
    w0 = w_in[0]
    take = lambda name: lax.slice_in_dim(w0, src[name][0], src[name][0] + src[name][1], axis=1)
    kiwi = _pad_cols(jnp.concatenate([take("ki"), take("wi")], axis=1), V7X_LANES)
    f_names = ("q", "qi", "hq", "hf", "hi", "k")
    f_parts = [take(n) for n in f_names] + [kiwi]
    cols_f, o = {}, 0
    for n, p in zip(f_names + ("kiwi",), f_parts):
        cols_f[n] = o
        o += p.shape[1]
    tn_f = 2816
    w_f = _pad_cols(jnp.concatenate(f_parts, axis=1), -(-o // tn_f) * tn_f).astype(BF16)
    b_names = ("gates", "hg", "v")
    cols_b, o = {}, 0
    for n in b_names:
        cols_b[n] = o
        o += widths[n]
    w_b = jnp.concatenate([take(n) for n in b_names], axis=1).astype(BF16)

    g1 = norm1_g[0].reshape(1, d)
    proj_f = _norm_matmul(xs, g1, sc1, sh1, w_f, F32, 512, tn_f, "in_proj_f32")
    proj_b = _norm_matmul(xs, g1, sc1, sh1, w_b, BF16, 512, 1792, "in_proj_bf16")

    q_r, k_r, qi_hm, ki_r, wi_r = _dsa_prep(proj_f, pos, idx_knorm_g[0], idx_knorm_b[0], cols_f, 512)
    topk = min(IDX_TOPK, s // 4)
    att = _dsa(qi_hm, wi_r, q_r, ki_r, k_r, proj_b, cols_b["v"], 128, 512, topk)

    lb_all = jnp.cumsum(jax.nn.softmax(hgrn_lb_logits.astype(F32), axis=0), axis=0)
    rec = _hgrn(proj_f, proj_b, lb_all[0], hgrn_norm_g[0], cols_f, cols_b)

    x1, h2, top_e, top_g, sel = _post_mixer(
        att, rec, proj_b, cols_b["gates"], xs,
        w_proj_att[0].astype(BF16), w_proj_hgrn[0].astype(BF16), w_out[0].astype(BF16),
        gt1, norm2_g[0].reshape(1, d), sc2, sh2, w_router[0], b_router[0].reshape(1, -1), 256)

    rank, counts = _expert_ranks(sel, 512)
    rb = MOE_ROW_BLOCK
    n_sb = s * MOE_TOPK // rb + N_EXPERTS
    cnt = counts[0].astype(I32)
    pcnt = (cnt + rb - 1) // rb * rb
    pend = jnp.cumsum(pcnt)
    pstart = pend - pcnt
    dest = jnp.take_along_axis(rank.astype(I32) + pstart[None, :], top_e, axis=1).reshape(-1)
    sb_row = jnp.arange(n_sb, dtype=I32) * rb
    sb_expert = jnp.clip(jnp.searchsorted(pend, sb_row, side="right"), 0, N_EXPERTS - 1).astype(I32)
    sb_valid = jnp.clip(cnt[sb_expert] - (sb_row - pstart[sb_expert]), 0, rb).astype(I32)
    n_used = (pend[-1] // rb).astype(I32).reshape(1)

    xb = _scatter_rows(dest, h2, n_sb * rb, 512)
    yb = _experts(sb_expert, sb_valid, n_used, xb, w1[0].astype(BF16), b1[0], w2[0].astype(BF16), b2[0],
                  rb, MOE_HIDDEN_TILE)
    out = _combine(dest, yb, x1, top_g, gt2, final_norm_g.reshape(1, d), 256)
    return out.reshape(bsz, s, d)
```

```python
import functools

import numpy as np
import jax
import jax.numpy as jnp
from jax import lax
from jax.experimental import pallas as pl
from jax.experimental.pallas import tpu as pltpu

F32 = jnp.float32
BF16 = jnp.bfloat16
I32 = jnp.int32

ATT_HEAD_DIM = 128
N_ATT_HEADS = 8
N_KV_HEADS = 2
ATT_ROPE_DIM = 32
IDX_HEAD_DIM = 64
N_IDX_HEADS = 16
IDX_ROPE_DIM = 16
IDX_TOPK = 256
ROPE_THETA = 500000.0
HGRN_DIM = 128
N_HGRN_HEADS = 8
N_EXPERTS = 32
MOE_TOPK = 4
SWIGLU_LIMIT = 7.0
SWIGLU_ALPHA = 1.702
NORM_EPS = 1e-5

V7X_LANES = 128
INT_MIN = -2147483648
MASKED_LOGIT = -1e30
EXP_CLAMP = 80.0
LOG2_E = 1.4426950408889634

MIB = 1024 * 1024


def _cparams(sem, vmem_mib):
    return pltpu.CompilerParams(dimension_semantics=sem, vmem_limit_bytes=vmem_mib * MIB)


def _sigmoid(x):
    return 1.0 / (1.0 + jnp.exp(-x))


def _dot(a, b):
    return jnp.dot(a, b, preferred_element_type=F32)


def _dot_nt(a, b):
    return lax.dot_general(a, b, (((1,), (1,)), ((), ())), preferred_element_type=F32)


def _dot_tn(a, b):
    return lax.dot_general(a, b, (((0,), (0,)), ((), ())), preferred_element_type=F32)


def _adaln_kernel(c_ref, w_ref, b_ref, o_ref):
    c = c_ref[...]
    cs = c * _sigmoid(c)
    o_ref[...] = jnp.dot(cs, w_ref[...], preferred_element_type=F32,
                         precision=lax.Precision.HIGHEST) + b_ref[...]


def _adaln(c, w, b):
    d, n = w.shape
    tn = 1024
    c8 = jnp.broadcast_to(c, (8, d))
    out = pl.pallas_call(
        _adaln_kernel,
        grid=(n // tn,),
        in_specs=[pl.BlockSpec((8, d), lambda j: (0, 0)),
                  pl.BlockSpec((d, tn), lambda j: (0, j)),
                  pl.BlockSpec((1, tn), lambda j: (0, j))],
        out_specs=pl.BlockSpec((8, tn), lambda j: (0, j)),
        out_shape=jax.ShapeDtypeStruct((8, n), F32),
        compiler_params=_cparams(("arbitrary",), 40),
        name="adaln",
    )(c8, w, b.reshape(1, n))
    return out[0:1]


def _norm_matmul_kernel(x_ref, g_ref, sc_ref, sh_ref, w_ref, o_ref, h_ref):
    @pl.when(pl.program_id(1) == 0)
    def _():
        x = x_ref[...]
        ms = jnp.mean(x * x, axis=-1, keepdims=True)
        y = x * lax.rsqrt(ms + NORM_EPS) * g_ref[...]
        h_ref[...] = (y * (1.0 + sc_ref[...]) + sh_ref[...]).astype(BF16)

    o_ref[...] = _dot(h_ref[...], w_ref[...]).astype(o_ref.dtype)


def _norm_matmul(x, g, sc, sh, w, out_dtype, tm, tn, name):
    s, d = x.shape
    n = w.shape[1]
    vec = pl.BlockSpec((1, d), lambda i, j: (0, 0))
    return pl.pallas_call(
        _norm_matmul_kernel,
        grid=(s // tm, n // tn),
        in_specs=[pl.BlockSpec((tm, d), lambda i, j: (i, 0)), vec, vec, vec,
                  pl.BlockSpec((d, tn), lambda i, j: (0, j))],
        out_specs=pl.BlockSpec((tm, tn), lambda i, j: (i, j)),
        out_shape=jax.ShapeDtypeStruct((s, n), out_dtype),
        scratch_shapes=[pltpu.VMEM((tm, d), BF16)],
        compiler_params=_cparams(("parallel", "arbitrary"), 56),
        name=name,
    )(x, g, sc, sh, w)


def _rope(x, a, bm, bp, half):
    n = x.shape[-1]
    return x * a + pltpu.roll(x, n - half, 1) * bm + pltpu.roll(x, half, 1) * bp


def _prep_kernel(q_ref, qi_ref, k_ref, kw_ref, ta_ref, ti_ref, kn_ref,
                 qo_ref, ko_ref, qio_ref, kio_ref, wio_ref):
    aa, abm, abp = ta_ref[0], ta_ref[1], ta_ref[2]
    ia, ibm, ibp = ti_ref[0], ti_ref[1], ti_ref[2]
    ah = ATT_ROPE_DIM // 2
    ih = IDX_ROPE_DIM // 2
    scale = ATT_HEAD_DIM ** -0.5 * LOG2_E
    for h in range(N_ATT_HEADS):
        sl = slice(h * ATT_HEAD_DIM, (h + 1) * ATT_HEAD_DIM)
        qo_ref[:, sl] = (_rope(q_ref[:, sl], aa, abm, abp, ah) * scale).astype(BF16)
    for h in range(N_KV_HEADS):
        sl = slice(h * ATT_HEAD_DIM, (h + 1) * ATT_HEAD_DIM)
        ko_ref[:, sl] = _rope(k_ref[:, sl], aa, abm, abp, ah).astype(BF16)
    for p in range(N_IDX_HEADS // 2):
        sl = slice(p * V7X_LANES, (p + 1) * V7X_LANES)
        y = _rope(qi_ref[:, sl], ia, ibm, ibp, ih).astype(BF16)
        qio_ref[2 * p] = y[:, :IDX_HEAD_DIM]
        qio_ref[2 * p + 1] = y[:, IDX_HEAD_DIM:]
    kw = kw_ref[...]
    lane = lax.broadcasted_iota(I32, kw.shape, 1)
    is_k = lane < IDX_HEAD_DIM
    mu = jnp.sum(jnp.where(is_k, kw, 0.0), axis=-1, keepdims=True) * (1.0 / IDX_HEAD_DIM)
    cen = kw - mu
    var = jnp.sum(jnp.where(is_k, cen * cen, 0.0), axis=-1, keepdims=True) * (1.0 / IDX_HEAD_DIM)
    kn = cen * lax.rsqrt(var + NORM_EPS) * kn_ref[0:1, :] + kn_ref[1:2, :]
    kn = _rope(kn, ia, ibm, ibp, ih)
    kio_ref[...] = kn[:, :IDX_HEAD_DIM].astype(BF16)
    wscale = N_IDX_HEADS ** -0.5 * IDX_HEAD_DIM ** -0.5
    wio_ref[...] = kw[:, IDX_HEAD_DIM:IDX_HEAD_DIM + N_IDX_HEADS] * wscale


def _rope_tables(pos, rot_dim, head_dim):
    half = rot_dim // 2
    inv_freq = jnp.power(ROPE_THETA, -jnp.arange(half, dtype=F32) * 2.0 / rot_dim)
    ang = pos.astype(F32)[:, None] * inv_freq
    cos, sin = jnp.cos(ang), jnp.sin(ang)
    s = pos.shape[0]
    rest = head_dim - rot_dim
    a = jnp.concatenate([cos, cos, jnp.ones((s, rest), F32)], axis=1)
    bm = jnp.concatenate([-sin, jnp.zeros((s, head_dim - half), F32)], axis=1)
    bp = jnp.concatenate([jnp.zeros((s, half), F32), sin, jnp.zeros((s, rest), F32)], axis=1)
    reps = V7X_LANES // head_dim
    return jnp.stack([jnp.tile(t, (1, reps)) for t in (a, bm, bp)])


def _dsa_prep(proj_f, pos, knorm_g, knorm_b, cols, tb):
    s = proj_f.shape[0]
    ta = _rope_tables(pos, ATT_ROPE_DIM, ATT_HEAD_DIM)
    ti = _rope_tables(pos, IDX_ROPE_DIM, IDX_HEAD_DIM)
    kn = jnp.zeros((2, V7X_LANES), F32)
    kn = kn.at[0, :IDX_HEAD_DIM].set(knorm_g).at[1, :IDX_HEAD_DIM].set(knorm_b)
    qw = N_ATT_HEADS * ATT_HEAD_DIM
    qiw = N_IDX_HEADS * IDX_HEAD_DIM
    kw = N_KV_HEADS * ATT_HEAD_DIM
    tab = pl.BlockSpec((3, tb, V7X_LANES), lambda i: (0, i, 0))
    return pl.pallas_call(
        _prep_kernel,
        grid=(s // tb,),
        in_specs=[pl.BlockSpec((tb, qw), lambda i: (i, cols["q"] // qw)),
                  pl.BlockSpec((tb, qiw), lambda i: (i, cols["qi"] // qiw)),
                  pl.BlockSpec((tb, kw), lambda i: (i, cols["k"] // kw)),
                  pl.BlockSpec((tb, V7X_LANES), lambda i: (i, cols["kiwi"] // V7X_LANES)),
                  tab, tab,
                  pl.BlockSpec((2, V7X_LANES), lambda i: (0, 0))],
        out_specs=[pl.BlockSpec((tb, qw), lambda i: (i, 0)),
                   pl.BlockSpec((tb, kw), lambda i: (i, 0)),
                   pl.BlockSpec((N_IDX_HEADS, tb, IDX_HEAD_DIM), lambda i: (0, i, 0)),
                   pl.BlockSpec((tb, IDX_HEAD_DIM), lambda i: (i, 0)),
                   pl.BlockSpec((tb, N_IDX_HEADS), lambda i: (i, 0))],
        out_shape=[jax.ShapeDtypeStruct((s, qw), BF16),
                   jax.ShapeDtypeStruct((s, kw), BF16),
                   jax.ShapeDtypeStruct((N_IDX_HEADS, s, IDX_HEAD_DIM), BF16),
                   jax.ShapeDtypeStruct((s, IDX_HEAD_DIM), BF16),
                   jax.ShapeDtypeStruct((s, N_IDX_HEADS), F32)],
        compiler_params=_cparams(("parallel",), 40),
        name="dsa_prep",
    )(proj_f, proj_f, proj_f, proj_f, ta, ti, kn)


def _order_key(x):
    b = lax.bitcast_convert_type(x, I32)
    return jnp.where(b < 0, b ^ jnp.int32(0x7FFFFFFF), b)


def _key_value(k):
    return lax.bitcast_convert_type(jnp.where(k < 0, k ^ jnp.int32(0x7FFFFFFF), k), F32)


KEY_NEG_INF = -2139095041
KEY_POS_INF = 2139095040
SEARCH_INTERP_ITERS = 12
SOFTMAX_MIN_SUM = 1e-25


def _lane_chunk_sum(m, tk):
    acc = m[:, 0:V7X_LANES]
    for c in range(1, tk // V7X_LANES):
        acc = acc + m[:, c * V7X_LANES:(c + 1) * V7X_LANES]
    return acc


def _dsa_kernel(qi_ref, wi_ref, q_ref, ki_ref, k_ref, v_ref, o_ref,
                key_ref, acc_ref, s_ref, m_ref, l_ref, kmax_ref, *, tq, tk, topk):
    i = pl.program_id(0)
    n_kt = (i * tq + tq + tk - 1) // tk
    qpos = i * tq + lax.broadcasted_iota(I32, (tq, 1), 0)
    wi = wi_ref[...]
    rep = N_ATT_HEADS // N_KV_HEADS

    @pl.when(i == 0)
    def _():
        for g in range(N_KV_HEADS):
            gs = slice(g * ATT_HEAD_DIM, (g + 1) * ATT_HEAD_DIM)

            def body(t, m):
                r0 = pl.multiple_of(t * tk, tk)
                kk = k_ref[pl.ds(r0, tk), gs].astype(F32)
                n2 = jnp.sum(kk * kk, axis=-1, keepdims=True)
                return jnp.maximum(m, jnp.max(n2, axis=0, keepdims=True))

            m = lax.fori_loop(0, k_ref.shape[0] // tk, body, jnp.zeros((1, 1), F32))
            kmax_ref[g] = jnp.broadcast_to(jnp.sqrt(m), (1, V7X_LANES))

    def score_tile(kt, carry):
        m1, m2 = carry
        k0 = pl.multiple_of(kt * tk, tk)
        kit = ki_ref[pl.ds(k0, tk), :]
        acc = jnp.zeros((tq, tk), F32)
        for h in range(N_IDX_HEADS):
            d = _dot_nt(qi_ref[h], kit)
            acc = acc + wi[:, h:h + 1] * jnp.maximum(d, 0.0)
        kpos = k0 + lax.broadcasted_iota(I32, (1, tk), 1)
        key = jnp.where(kpos <= qpos, _order_key(acc), jnp.int32(INT_MIN))
        key_ref[kt] = key
        for c in range(tk // V7X_LANES):
            x = key[:, c * V7X_LANES:(c + 1) * V7X_LANES]
            m2 = jnp.maximum(m2, jnp.minimum(m1, x))
            m1 = jnp.maximum(m1, x)
        return m1, m2

    lowest = jnp.full((tq, V7X_LANES), INT_MIN, I32)
    m1, m2 = lax.fori_loop(0, n_kt, score_tile, (lowest, lowest))
    row_max = jnp.max(m1, axis=-1, keepdims=True)

    def count_ge(thr):
        def body(kt, acc):
            m = jnp.where(key_ref[kt] >= thr, 1, 0).astype(I32)
            return acc + _lane_chunk_sum(m, tk)
        acc = lax.fori_loop(0, n_kt, body, jnp.zeros((tq, V7X_LANES), I32))
        return jnp.sum(acc, axis=-1, keepdims=True)

    def search_cond(st):
        _, lo, hi, _, _ = st
        return jnp.max(jnp.where(lo + 1 < hi, 1, 0)) > 0

    def search_body(st):
        it, lo, hi, cnt_lo, cnt_hi = st
        active = lo + 1 < hi
        vlo = _key_value(lo)
        vhi = _key_value(hi)
        la = jnp.log(cnt_lo.astype(F32))
        lb = jnp.log(jnp.maximum(cnt_hi.astype(F32), 0.5))
        frac = (la - float(np.log(topk - 0.5))) / (la - lb)
        interp = _order_key(vlo + frac * (vhi - vlo))
        bisect = (lo & hi) + ((lo ^ hi) >> 1)
        use_interp = ((lo > KEY_NEG_INF) & (hi <= KEY_POS_INF)
                      & ((it < SEARCH_INTERP_ITERS) | ((it & 1) == 0)))
        mid = jnp.where(use_interp, interp, bisect)
        mid = jnp.minimum(jnp.maximum(mid, lo + 1), hi - 1)
        mid = jnp.where(active, mid, lo)
        cnt = count_ge(mid)
        up = active & (cnt >= topk)
        down = active & (cnt < topk)
        hi = jnp.where(up & (cnt == topk), mid + 1, jnp.where(down, mid, hi))
        lo = jnp.where(up, mid, lo)
        cnt_lo = jnp.where(up, cnt, cnt_lo)
        cnt_hi = jnp.where(down, cnt, cnt_hi)
        return it + 1, lo, hi, cnt_lo, cnt_hi

    lo0 = jnp.min(m2, axis=-1, keepdims=True)
    cnt0 = count_ge(lo0)
    hi0 = jnp.where(cnt0 == topk, lo0 + 1, row_max + 1)
    zero = jnp.zeros((tq, 1), I32)
    _, thr, _, cnt_thr, _ = lax.while_loop(search_cond, search_body, (zero, lo0, hi0, cnt0, zero))

    tie_rows = (cnt_thr > topk) & (thr > INT_MIN)

    @pl.when(jnp.max(jnp.where(tie_rows, 1, 0)) > 0)
    def _():
        need = topk - count_ge(thr + 1)
        a = lax.broadcasted_iota(I32, (tk, tk), 0)
        b = lax.broadcasted_iota(I32, (tk, tk), 1)
        before = jnp.where(a < b, 1.0, 0.0).astype(BF16)

        def body(kt, seen):
            key = key_ref[kt]
            eq = (key == thr) & tie_rows
            eqf = jnp.where(eq, 1.0, 0.0)
            rank = seen + _dot(eqf.astype(BF16), before)
            drop = eq & (rank >= need.astype(F32))
            key_ref[kt] = jnp.where(drop, jnp.int32(INT_MIN), key)
            return seen + jnp.sum(eqf, axis=-1, keepdims=True)

        lax.fori_loop(0, n_kt, body, jnp.zeros((tq, 1), F32))

    thr = jnp.maximum(thr, INT_MIN + 1)

    acc_ref[...] = jnp.zeros(acc_ref.shape, F32)
    for h in range(N_ATT_HEADS):
        qh = q_ref[:, h * ATT_HEAD_DIM:(h + 1) * ATT_HEAD_DIM].astype(F32)
        qn = jnp.sqrt(jnp.sum(qh * qh, axis=-1, keepdims=True))
        m_ref[h] = qn * kmax_ref[h // rep][:, 0:1]
    ones_col = jnp.where(lax.broadcasted_iota(I32, (tk, V7X_LANES), 1) == 0, 1.0, 0.0).astype(BF16)

    def attend_tile(kt, carry):
        k0 = pl.multiple_of(kt * tk, tk)
        keep = jnp.where(key_ref[kt] >= thr, 1.0, 0.0)
        for h in range(N_ATT_HEADS):
            gs = slice((h // rep) * ATT_HEAD_DIM, (h // rep + 1) * ATT_HEAD_DIM)
            qh = q_ref[:, h * ATT_HEAD_DIM:(h + 1) * ATT_HEAD_DIM]
            s_ref[h] = _dot_nt(qh, k_ref[pl.ds(k0, tk), gs])
        for h in range(N_ATT_HEADS):
            gs = slice((h // rep) * ATT_HEAD_DIM, (h // rep + 1) * ATT_HEAD_DIM)
            p = jnp.exp2(s_ref[h] - m_ref[h]) * keep
            v_aug = jnp.concatenate([v_ref[pl.ds(k0, tk), gs], ones_col], axis=1)
            acc_ref[h] += _dot(p.astype(BF16), v_aug)
        return carry

    lax.fori_loop(0, n_kt, attend_tile, 0)
    underflow = jnp.int32(0)
    for h in range(N_ATT_HEADS):
        l = acc_ref[h, :, ATT_HEAD_DIM:ATT_HEAD_DIM + 1]
        underflow = jnp.maximum(underflow, jnp.max(jnp.where(l >= SOFTMAX_MIN_SUM, 0, 1)))
        o_ref[:, h * ATT_HEAD_DIM:(h + 1) * ATT_HEAD_DIM] = (acc_ref[h, :, 0:ATT_HEAD_DIM] / l).astype(o_ref.dtype)

    @pl.when(underflow > 0)
    def _():
        _attend_online(q_ref, k_ref, v_ref, o_ref, key_ref, acc_ref, m_ref, l_ref, thr, n_kt, tk)


def _attend_online(q_ref, k_ref, v_ref, o_ref, key_ref, acc_ref, m_ref, l_ref, thr, n_kt, tk):
    rep = N_ATT_HEADS // N_KV_HEADS
    d = ATT_HEAD_DIM
    m_ref[...] = jnp.full(m_ref.shape, MASKED_LOGIT, F32)
    l_ref[...] = jnp.zeros(l_ref.shape, F32)
    acc_ref[...] = jnp.zeros(acc_ref.shape, F32)

    def online_tile(kt, carry):
        k0 = pl.multiple_of(kt * tk, tk)
        sel = key_ref[kt] >= thr
        for g in range(N_KV_HEADS):
            kg = k_ref[pl.ds(k0, tk), g * d:(g + 1) * d]
            vg = v_ref[pl.ds(k0, tk), g * d:(g + 1) * d]
            for r in range(rep):
                h = g * rep + r
                s = jnp.where(sel, _dot_nt(q_ref[:, h * d:(h + 1) * d], kg), MASKED_LOGIT)
                m_old = m_ref[h]
                m_new = jnp.maximum(m_old, jnp.max(s, axis=-1, keepdims=True))
                alpha = jnp.exp2(m_old - m_new)
                p = jnp.exp2(s - m_new)
                l_ref[h] = alpha * l_ref[h] + jnp.sum(p, axis=-1, keepdims=True)
                acc_ref[h, :, 0:d] = alpha * acc_ref[h, :, 0:d] + _dot(p.astype(BF16), vg)
                m_ref[h] = m_new
        return carry

    lax.fori_loop(0, n_kt, online_tile, 0)
    for h in range(N_ATT_HEADS):
        o_ref[:, h * d:(h + 1) * d] = (acc_ref[h, :, 0:d] / l_ref[h]).astype(o_ref.dtype)


def _dsa(qi_hm, wi, q, ki, k, v_src, v_col, tq, tk, topk):
    s = q.shape[0]
    kw = N_KV_HEADS * ATT_HEAD_DIM
    qw = N_ATT_HEADS * ATT_HEAD_DIM
    whole = lambda shape, idx: pl.BlockSpec(shape, idx, pipeline_mode=pl.Buffered(1))
    kern = functools.partial(_dsa_kernel, tq=tq, tk=tk, topk=topk)
    return pl.pallas_call(
        kern,
        grid=(s // tq,),
        in_specs=[pl.BlockSpec((N_IDX_HEADS, tq, IDX_HEAD_DIM), lambda i: (0, i, 0)),
                  pl.BlockSpec((tq, N_IDX_HEADS), lambda i: (i, 0)),
                  pl.BlockSpec((tq, qw), lambda i: (i, 0)),
                  whole((s, IDX_HEAD_DIM), lambda i: (0, 0)),
                  whole((s, kw), lambda i: (0, 0)),
                  whole((s, kw), lambda i: (0, v_col // kw))],
        out_specs=pl.BlockSpec((tq, qw), lambda i: (i, 0)),
        out_shape=jax.ShapeDtypeStruct((s, qw), BF16),
        scratch_shapes=[pltpu.VMEM((s // tk, tq, tk), I32),
                        pltpu.VMEM((N_ATT_HEADS, tq, 2 * ATT_HEAD_DIM), F32),
                        pltpu.VMEM((N_ATT_HEADS, tq, tk), F32),
                        pltpu.VMEM((N_ATT_HEADS, tq, 1), F32),
                        pltpu.VMEM((N_ATT_HEADS, tq, 1), F32),
                        pltpu.VMEM((N_KV_HEADS, 1, V7X_LANES), F32)],
        compiler_params=_cparams(("arbitrary",), 56),
        name="dsa",
    )(qi_hm, wi, q, ki, k, v_src)


HGRN_BLOCK = 256
HGRN_LEVELS = (32, 64, 128, 256)


def _hgrn_constants():
    tb = HGRN_BLOCK
    t = np.arange(tb)[:, None]
    u = np.arange(tb)[None, :]
    lower = (u <= t).astype(np.float32)
    mats = [lower]
    c0 = HGRN_LEVELS[0]
    ref0 = c0 * (t // c0) + c0 // 2 - 1
    mats.append(lower - (u <= ref0))
    for c in HGRN_LEVELS[1:-1]:
        ref = c * (t // c) + c // 2 - 1
        mats.append(lower - (u <= ref))
    level = np.full((tb, tb), -1.0, np.float32)
    level[(t // c0 == u // c0) & (u <= t)] = 0.0
    for li, c in enumerate(HGRN_LEVELS[1:], start=1):
        sel = (t // c == u // c) & (t % c >= c // 2) & (u % c < c // 2)
        level[sel] = float(li)
    return np.concatenate(mats, axis=0), level


def _hgrn_kernel(q_ref, f_ref, i_ref, g_ref, lb_ref, gn_ref, mats_ref, lvl_ref, o_ref, st_ref):
    tb = HGRN_BLOCK
    d = HGRN_DIM

    @pl.when(pl.program_id(1) == 0)
    def _():
        st_ref[...] = jnp.zeros(st_ref.shape, F32)

    q = q_ref[...]
    qf = q * _sigmoid(q)
    lb = lb_ref[0]
    f = lb + (1.0 - lb) * _sigmoid(f_ref[...])
    logf = jnp.log(f)
    kf = 1.0 - f
    v = i_ref[...].astype(BF16)

    g1 = logf.astype(BF16)
    r1 = logf - g1.astype(F32)
    g2 = r1.astype(BF16)
    g3 = (r1 - g2.astype(F32)).astype(BF16)
    sums = _dot(mats_ref[...], jnp.concatenate([g1, g2, g3], axis=1))
    sums = sums[:, 0:d] + sums[:, d:2 * d] + sums[:, 2 * d:3 * d]
    n_mid = len(HGRN_LEVELS) - 1
    cum = sums[0:tb]
    rel = [sums[(l + 1) * tb:(l + 2) * tb] for l in range(n_mid)]
    rel.append(cum - cum[tb // 2 - 1:tb // 2, :])

    lvl = lvl_ref[...]
    amat = jnp.zeros((tb, tb), F32)
    for l, e in enumerate(rel):
        cap = EXP_CLAMP if l == 0 else 0.0
        qs = (qf * jnp.exp(jnp.minimum(e, cap))).astype(BF16)
        ks = (kf * jnp.exp(jnp.minimum(-e, cap))).astype(BF16)
        amat = jnp.where(lvl == float(l), _dot_nt(qs, ks), amat)

    st = st_ref[...]
    o = _dot(amat.astype(BF16), v) + _dot_nt((qf * jnp.exp(cum)).astype(BF16), st.astype(BF16))
    last = cum[tb - 1:tb, :]
    ks_state = (kf * jnp.exp(last - cum)).astype(BF16)
    st_ref[...] = st * jnp.exp(last) + _dot_tn(v, ks_state)

    ms = jnp.mean(o * o, axis=-1, keepdims=True)
    y = o * lax.rsqrt(ms + NORM_EPS) * gn_ref[...]
    gate = g_ref[...].astype(F32)
    o_ref[...] = (y * (gate * _sigmoid(gate))).astype(o_ref.dtype)


def _hgrn(proj_f, proj_b, lb, gn, cols_f, cols_b):
    s = proj_f.shape[0]
    tb = HGRN_BLOCK
    d = HGRN_DIM
    mats_np, lvl_np = _hgrn_constants()
    mats = jnp.asarray(mats_np, BF16)
    lvl = jnp.asarray(lvl_np, F32)
    col = lambda off: (lambda h, j: (j, off // d + h))
    const = lambda shape: pl.BlockSpec(shape, lambda h, j: (0,) * len(shape))
    return pl.pallas_call(
        _hgrn_kernel,
        grid=(N_HGRN_HEADS, s // tb),
        in_specs=[pl.BlockSpec((tb, d), col(cols_f["hq"])),
                  pl.BlockSpec((tb, d), col(cols_f["hf"])),
                  pl.BlockSpec((tb, d), col(cols_f["hi"])),
                  pl.BlockSpec((tb, d), col(cols_b["hg"])),
                  pl.BlockSpec((1, 1, d), lambda h, j: (h, 0, 0)),
                  const((1, d)),
                  const(mats_np.shape),
                  const(lvl_np.shape)],
        out_specs=pl.BlockSpec((tb, d), lambda h, j: (j, h)),
        out_shape=jax.ShapeDtypeStruct((s, N_HGRN_HEADS * d), BF16),
        scratch_shapes=[pltpu.VMEM((d, d), F32)],
        compiler_params=_cparams(("parallel", "arbitrary"), 32),
        name="hgrn",
    )(proj_f, proj_f, proj_f, proj_b, lb.reshape(N_HGRN_HEADS, 1, d), gn.reshape(1, d), mats, lvl)


def _post_kernel(att_ref, rec_ref, ga_ref, gr_ref, x_ref, wpa_ref, wph_ref, wo_ref,
                 gt_ref, g2_ref, sc_ref, sh_ref, wr_ref, br_ref,
                 x1_ref, h2_ref, te_ref, tg_ref, sel_ref):
    a = _dot(att_ref[...], wpa_ref[...])
    r = _dot(rec_ref[...], wph_ref[...])
    merged = _sigmoid(ga_ref[...].astype(F32)) * a + _sigmoid(gr_ref[...].astype(F32)) * r
    mix = _dot(merged.astype(BF16), wo_ref[...])
    x1 = x_ref[...] + gt_ref[...] * mix
    x1_ref[...] = x1
    ms = jnp.mean(x1 * x1, axis=-1, keepdims=True)
    h2 = x1 * lax.rsqrt(ms + NORM_EPS) * g2_ref[...]
    h2 = h2 * (1.0 + sc_ref[...]) + sh_ref[...]
    h2_ref[...] = h2
    logits = jnp.dot(h2, wr_ref[...], preferred_element_type=F32,
                     precision=lax.Precision.HIGHEST) + br_ref[...]
    ne = logits.shape[-1]
    lane = lax.broadcasted_iota(I32, logits.shape, 1)
    work = logits
    vals, idxs = [], []
    sel = jnp.zeros(logits.shape, F32)
    for _ in range(MOE_TOPK):
        m = jnp.max(work, axis=-1, keepdims=True)
        idx = jnp.min(jnp.where(work == m, lane, ne), axis=-1, keepdims=True)
        hit = lane == idx
        vals.append(m)
        idxs.append(idx)
        sel = jnp.where(hit, 1.0, sel)
        work = jnp.where(hit, -jnp.inf, work)
    es = [jnp.exp(vk - vals[0]) for vk in vals]
    denom = es[0]
    for e in es[1:]:
        denom = denom + e
    lane4 = lax.broadcasted_iota(I32, te_ref.shape, 1)
    te = jnp.zeros(te_ref.shape, I32)
    tg = jnp.zeros(tg_ref.shape, F32)
    for kk in range(MOE_TOPK):
        te = jnp.where(lane4 == kk, idxs[kk], te)
        tg = jnp.where(lane4 == kk, es[kk] / denom, tg)
    te_ref[...] = te
    tg_ref[...] = tg
    sel_ref[...] = sel.astype(sel_ref.dtype)


def _post_mixer(att, rec, proj_b, gates_col, x, wpa, wph, wo, gt1, g2, sc2, sh2, wr, br, tm):
    s, d = x.shape
    aw = att.shape[1]
    rw = rec.shape[1]
    ne = wr.shape[1]
    const = lambda shape: pl.BlockSpec(shape, lambda i: (0,) * len(shape), pipeline_mode=pl.Buffered(1))
    row = lambda w: pl.BlockSpec((tm, w), lambda i: (i, 0))
    gcol = gates_col // d
    return pl.pallas_call(
        _post_kernel,
        grid=(s // tm,),
        in_specs=[row(aw), row(rw),
                  pl.BlockSpec((tm, d), lambda i: (i, gcol)),
                  pl.BlockSpec((tm, d), lambda i: (i, gcol + 1)),
                  row(d),
                  const((aw, d)), const((rw, d)), const((d, d)),
                  const((1, d)), const((1, d)), const((1, d)), const((1, d)),
                  const((d, ne)), const((1, ne))],
        out_specs=[row(d), row(d), row(MOE_TOPK), row(MOE_TOPK), row(ne)],
        out_shape=[jax.ShapeDtypeStruct((s, d), F32),
                   jax.ShapeDtypeStruct((s, d), F32),
                   jax.ShapeDtypeStruct((s, MOE_TOPK), I32),
                   jax.ShapeDtypeStruct((s, MOE_TOPK), F32),
                   jax.ShapeDtypeStruct((s, ne), BF16)],
        compiler_params=_cparams(("parallel",), 56),
        name="post_mixer",
    )(att, rec, proj_b, proj_b, x, wpa, wph, wo, gt1, g2, sc2, sh2, wr, br)


def _rank_kernel(sel_ref, rank_ref, cnt_ref, carry_ref):
    @pl.when(pl.program_id(0) == 0)
    def _():
        carry_ref[...] = jnp.zeros(carry_ref.shape, F32)

    sel = sel_ref[...]
    tt = sel.shape[0]
    a = lax.broadcasted_iota(I32, (tt, tt), 0)
    b = lax.broadcasted_iota(I32, (tt, tt), 1)
    before = jnp.where(b < a, 1.0, 0.0).astype(BF16)
    carry = carry_ref[...]
    rank_ref[...] = _dot(before, sel) + carry
    carry = carry + jnp.sum(sel.astype(F32), axis=0, keepdims=True)
    carry_ref[...] = carry
    cnt_ref[...] = carry


def _expert_ranks(sel, tt):
    s, ne = sel.shape
    return pl.pallas_call(
        _rank_kernel,
        grid=(s // tt,),
        in_specs=[pl.BlockSpec((tt, ne), lambda i: (i, 0))],
        out_specs=[pl.BlockSpec((tt, ne), lambda i: (i, 0)),
                   pl.BlockSpec((1, ne), lambda i: (0, 0))],
        out_shape=[jax.ShapeDtypeStruct((s, ne), F32),
                   jax.ShapeDtypeStruct((1, ne), F32)],
        scratch_shapes=[pltpu.VMEM((1, ne), F32)],
        compiler_params=_cparams(("arbitrary",), 16),
        name="expert_ranks",
    )(sel)


def _row_copy(src_ref, src_row, dst_ref, dst_row, sem):
    return pltpu.make_async_copy(src_ref.at[pl.ds(src_row, 1)], dst_ref.at[pl.ds(dst_row, 1)], sem)


def _scatter_kernel(dest_ref, h_ref, xb_in_ref, xb_ref, sem, *, tt):
    del xb_in_ref
    base = pl.program_id(0) * tt * MOE_TOPK

    def issue(t, c):
        for kk in range(MOE_TOPK):
            _row_copy(h_ref, t, xb_ref, dest_ref[base + t * MOE_TOPK + kk], sem).start()
        return c

    lax.fori_loop(0, tt, issue, 0)

    def drain(t, c):
        for kk in range(MOE_TOPK):
            _row_copy(h_ref, 0, xb_ref, 0, sem).wait()
        return c

    lax.fori_loop(0, tt, drain, 0)


def _scatter_rows(dest_flat, h2, n_rows, tt):
    s, d = h2.shape
    xb0 = jnp.zeros((n_rows, d), h2.dtype)
    grid_spec = pltpu.PrefetchScalarGridSpec(
        num_scalar_prefetch=1,
        grid=(s // tt,),
        in_specs=[pl.BlockSpec((tt, d), lambda i, dest: (i, 0)),
                  pl.BlockSpec(memory_space=pl.ANY)],
        out_specs=pl.BlockSpec(memory_space=pl.ANY),
        scratch_shapes=[pltpu.SemaphoreType.DMA(())],
    )
    return pl.pallas_call(
        functools.partial(_scatter_kernel, tt=tt),
        grid_spec=grid_spec,
        out_shape=jax.ShapeDtypeStruct((n_rows, d), h2.dtype),
        input_output_aliases={2: 0},
        compiler_params=_cparams(("arbitrary",), 32),
        name="moe_scatter",
    )(dest_flat, h2, xb0)


def _expert_kernel(sbe_ref, sbv_ref, nused_ref, x_ref, w1g_ref, w1l_ref, b1g_ref, b1l_ref,
                   w2_ref, b2_ref, y_ref):
    sb = pl.program_id(0)
    c = pl.program_id(1)

    @pl.when(sb < nused_ref[0])
    def _():
        rows = lax.broadcasted_iota(I32, (x_ref.shape[0], 1), 0)
        x = jnp.where(rows < sbv_ref[sb], x_ref[...], 0.0).astype(BF16)
        ug = _dot(x, w1g_ref[0]) + b1g_ref[0]
        ul = _dot(x, w1l_ref[0]) + b1l_ref[0]
        ug = jnp.minimum(ug, SWIGLU_LIMIT)
        ul = jnp.clip(ul, -SWIGLU_LIMIT, SWIGLU_LIMIT)
        act = ug * _sigmoid(SWIGLU_ALPHA * ug) * (ul + 1.0)
        part = _dot(act.astype(BF16), w2_ref[0])

        @pl.when(c == 0)
        def _():
            y_ref[...] = part + b2_ref[0]

        @pl.when(c > 0)
        def _():
            y_ref[...] += part

    @pl.when((sb >= nused_ref[0]) & (c == 0))
    def _():
        y_ref[...] = jnp.zeros(y_ref.shape, y_ref.dtype)


def _experts(sb_expert, sb_valid, n_used, xb, w1, b1, w2, b2, rb, th):
    n_rows, d = xb.shape
    ne, _, two_de = w1.shape
    de = two_de // 2
    nch = de // th
    n_sb = n_rows // rb
    blk = lambda sb, nu: jnp.minimum(sb, nu[0] - 1)
    exp = lambda sb, e, nu: e[blk(sb, nu)]
    chk = lambda sb, c, nu: jnp.where(sb < nu[0], c, nch - 1)
    grid_spec = pltpu.PrefetchScalarGridSpec(
        num_scalar_prefetch=3,
        grid=(n_sb, nch),
        in_specs=[pl.BlockSpec((rb, d), lambda sb, c, e, v, nu: (blk(sb, nu), 0)),
                  pl.BlockSpec((1, d, th), lambda sb, c, e, v, nu: (exp(sb, e, nu), 0, chk(sb, c, nu))),
                  pl.BlockSpec((1, d, th), lambda sb, c, e, v, nu: (exp(sb, e, nu), 0, nch + chk(sb, c, nu))),
                  pl.BlockSpec((1, 1, th), lambda sb, c, e, v, nu: (exp(sb, e, nu), 0, chk(sb, c, nu))),
                  pl.BlockSpec((1, 1, th), lambda sb, c, e, v, nu: (exp(sb, e, nu), 0, nch + chk(sb, c, nu))),
                  pl.BlockSpec((1, th, d), lambda sb, c, e, v, nu: (exp(sb, e, nu), chk(sb, c, nu), 0)),
                  pl.BlockSpec((1, 1, d), lambda sb, c, e, v, nu: (exp(sb, e, nu), 0, 0))],
        out_specs=pl.BlockSpec((rb, d), lambda sb, c, e, v, nu: (sb, 0)),
    )
    return pl.pallas_call(
        _expert_kernel,
        grid_spec=grid_spec,
        out_shape=jax.ShapeDtypeStruct((n_rows, d), F32),
        compiler_params=_cparams(("arbitrary", "arbitrary"), 56),
        name="moe_experts",
    )(sb_expert, sb_valid, n_used, xb, w1, w1, b1.reshape(ne, 1, two_de), b1.reshape(ne, 1, two_de),
      w2, b2.reshape(ne, 1, d))


def _combine_kernel(dest_ref, yb_ref, x1_ref, tg_ref, gt_ref, gn_ref, o_ref, buf_ref, sem, *, tt):
    base = pl.program_id(0) * tt * MOE_TOPK

    def issue(t, c):
        for kk in range(MOE_TOPK):
            _row_copy(yb_ref, dest_ref[base + t * MOE_TOPK + kk], buf_ref.at[kk], t, sem).start()
        return c

    lax.fori_loop(0, tt, issue, 0)

    def drain(t, c):
        for kk in range(MOE_TOPK):
            _row_copy(yb_ref, 0, buf_ref.at[kk], 0, sem).wait()
        return c

    lax.fori_loop(0, tt, drain, 0)

    tg = tg_ref[...]
    moe = tg[:, 0:1] * buf_ref[0]
    for kk in range(1, MOE_TOPK):
        moe = moe + tg[:, kk:kk + 1] * buf_ref[kk]
    x2 = x1_ref[...] + gt_ref[...] * moe
    ms = jnp.mean(x2 * x2, axis=-1, keepdims=True)
    o_ref[...] = x2 * lax.rsqrt(ms + NORM_EPS) * gn_ref[...]


def _combine(dest_flat, yb, x1, tg, gt2, gn, tt):
    s, d = x1.shape
    grid_spec = pltpu.PrefetchScalarGridSpec(
        num_scalar_prefetch=1,
        grid=(s // tt,),
        in_specs=[pl.BlockSpec(memory_space=pl.ANY),
                  pl.BlockSpec((tt, d), lambda i, dest: (i, 0)),
                  pl.BlockSpec((tt, MOE_TOPK), lambda i, dest: (i, 0)),
                  pl.BlockSpec((1, d), lambda i, dest: (0, 0)),
                  pl.BlockSpec((1, d), lambda i, dest: (0, 0))],
        out_specs=pl.BlockSpec((tt, d), lambda i, dest: (i, 0)),
        scratch_shapes=[pltpu.VMEM((MOE_TOPK, tt, d), F32),
                        pltpu.SemaphoreType.DMA(())],
    )
    return pl.pallas_call(
        functools.partial(_combine_kernel, tt=tt),
        grid_spec=grid_spec,
        out_shape=jax.ShapeDtypeStruct((s, d), F32),
        compiler_params=_cparams(("arbitrary",), 40),
        name="moe_combine",
    )(dest_flat, yb, x1, tg, gt2, gn)


MOE_ROW_BLOCK = 512
MOE_HIDDEN_TILE = 1024


def _pad_cols(w, n):
    return jnp.pad(w, ((0, 0), (0, n - w.shape[1])))


def kernel(x, c, positions, w_ada, b_ada, norm1_g, w_in, idx_knorm_g, idx_knorm_b, hgrn_lb_logits,
           hgrn_norm_g, w_proj_att, w_proj_hgrn, w_out, norm2_g, w_router, b_router, w1, b1, w2, b2,
           final_norm_g):
    bsz, s, d = x.shape
    assert bsz == 1 and w_ada.shape[0] == 1
    xs = x.reshape(s, d)
    pos = positions.reshape(s)

    mod = _adaln(c, w_ada[0], b_ada[0])
    sh1, sc1, gt1, sh2, sc2, gt2 = [mod[:, i * d:(i + 1) * d] for i in range(6)]

    widths = dict(q=N_ATT_HEADS * ATT_HEAD_DIM, k=N_KV_HEADS * ATT_HEAD_DIM, v=N_KV_HEADS * ATT_HEAD_DIM,
                  qi=N_IDX_HEADS * IDX_HEAD_DIM, ki=IDX_HEAD_DIM, wi=N_IDX_HEADS,
                  hq=N_HGRN_HEADS * HGRN_DIM, hf=N_HGRN_HEADS * HGRN_DIM, hi=N_HGRN_HEADS * HGRN_DIM,
                  hg=N_HGRN_HEADS * HGRN_DIM, gates=2 * d)
    src, off = {}, 0
    for name in ("q", "k", "v", "qi", "ki", "wi", "hq", "hf", "hi", "hg", "gates"):
        src[name] = (off, widths[name])
        off += widths[name]
    w0 = w_in[0]
    take = lambda name: lax.slice_in_dim(w0, src[name][0], src[name][0] + src[name][1], axis=1)
    kiwi = _pad_cols(jnp.concatenate([take("ki"), take("wi")], axis=1), V7X_LANES)
    f_names = ("q", "qi", "hq", "hf", "hi", "k")
    f_parts = [take(n) for n in f_names] + [kiwi]
    cols_f, o = {}, 0
    for n, p in zip(f_names + ("kiwi",), f_parts):
        cols_f[n] = o
        o += p.shape[1]
    tn_f = 2816
    w_f = _pad_cols(jnp.concatenate(f_parts, axis=1), -(-o // tn_f) * tn_f).astype(BF16)
    b_names = ("gates", "hg", "v")
    cols_b, o = {}, 0
    for n in b_names:
        cols_b[n] = o
        o += widths[n]
    w_b = jnp.concatenate([take(n) for n in b_names], axis=1).astype(BF16)

    g1 = norm1_g[0].reshape(1, d)
    proj_f = _norm_matmul(xs, g1, sc1, sh1, w_f, F32, 512, tn_f, "in_proj_f32")
    proj_b = _norm_matmul(xs, g1, sc1, sh1, w_b, BF16, 512, 1792, "in_proj_bf16")

    q_r, k_r, qi_hm, ki_r, wi_r = _dsa_prep(proj_f, pos, idx_knorm_g[0], idx_knorm_b[0], cols_f, 512)
    topk = min(IDX_TOPK, s // 4)
    att = _dsa(qi_hm, wi_r, q_r, ki_r, k_r, proj_b, cols_b["v"], 128, 1024, topk)

    lb_all = jnp.cumsum(jax.nn.softmax(hgrn_lb_logits.astype(F32), axis=0), axis=0)
    rec = _hgrn(proj_f, proj_b, lb_all[0], hgrn_norm_g[0], cols_f, cols_b)

    x1, h2, top_e, top_g, sel = _post_mixer(
        att, rec, proj_b, cols_b["gates"], xs,
        w_proj_att[0].astype(BF16), w_proj_hgrn[0].astype(BF16), w_out[0].astype(BF16),
        gt1, norm2_g[0].reshape(1, d), sc2, sh2, w_router[0], b_router[0].reshape(1, -1), 256)

    rank, counts = _expert_ranks(sel, 512)
    rb = MOE_ROW_BLOCK
    n_sb = s * MOE_TOPK // rb + N_EXPERTS
    cnt = counts[0].astype(I32)
    pcnt = (cnt + rb - 1) // rb * rb
    pend = jnp.cumsum(pcnt)
    pstart = pend - pcnt
    dest = jnp.take_along_axis(rank.astype(I32) + pstart[None, :], top_e, axis=1).reshape(-1)
    sb_row = jnp.arange(n_sb, dtype=I32) * rb
    sb_expert = jnp.minimum(jnp.sum((pend[None, :] <= sb_row[:, None]).astype(I32), axis=1), N_EXPERTS - 1)
    sb_valid = jnp.clip(cnt[sb_expert] - (sb_row - pstart[sb_expert]), 0, rb).astype(I32)
    n_used = (pend[-1] // rb).astype(I32).reshape(1)

    xb = _scatter_rows(dest, h2, n_sb * rb, 512)
    yb = _experts(sb_expert, sb_valid, n_used, xb, w1[0].astype(BF16), b1[0], w2[0].astype(BF16), b2[0],
                  rb, MOE_HIDDEN_TILE)
    out = _combine(dest, yb, x1, top_g, gt2, final_norm_g.reshape(1, d), 256)
    return out.reshape(bsz, s, d)
```

```python
import functools

import numpy as np
import jax
import jax.numpy as jnp
from jax import lax
from jax.experimental import pallas as pl
from jax.experimental.pallas import tpu as pltpu

F32 = jnp.float32
BF16 = jnp.bfloat16
I32 = jnp.int32

ATT_HEAD_DIM = 128
N_ATT_HEADS = 8
N_KV_HEADS = 2
ATT_ROPE_DIM = 32
IDX_HEAD_DIM = 64
N_IDX_HEADS = 16
IDX_ROPE_DIM = 16
IDX_TOPK = 256
ROPE_THETA = 500000.0
HGRN_DIM = 128
N_HGRN_HEADS = 8
N_EXPERTS = 32
MOE_TOPK = 4
SWIGLU_LIMIT = 7.0
SWIGLU_ALPHA = 1.702
NORM_EPS = 1e-5

V7X_LANES = 128
INT_MIN = -2147483648
MASKED_LOGIT = -1e30
EXP_CLAMP = 80.0
LOG2_E = 1.4426950408889634

MIB = 1024 * 1024


def _cparams(sem, vmem_mib):
    return pltpu.CompilerParams(dimension_semantics=sem, vmem_limit_bytes=vmem_mib * MIB)


def _sigmoid(x):
    return 1.0 / (1.0 + jnp.exp(-x))


def _dot(a, b):
    return jnp.dot(a, b, preferred_element_type=F32)


def _dot_nt(a, b):
    return lax.dot_general(a, b, (((1,), (1,)), ((), ())), preferred_element_type=F32)


def _dot_tn(a, b):
    return lax.dot_general(a, b, (((0,), (0,)), ((), ())), preferred_element_type=F32)


def _adaln_kernel(c_ref, w_ref, b_ref, o_ref):
    c = c_ref[...]
    cs = c * _sigmoid(c)
    o_ref[...] = jnp.dot(cs, w_ref[...], preferred_element_type=F32,
                         precision=lax.Precision.HIGHEST) + b_ref[...]


def _adaln(c, w, b):
    d, n = w.shape
    tn = 1024
    c8 = jnp.broadcast_to(c, (8, d))
    out = pl.pallas_call(
        _adaln_kernel,
        grid=(n // tn,),
        in_specs=[pl.BlockSpec((8, d), lambda j: (0, 0)),
                  pl.BlockSpec((d, tn), lambda j: (0, j)),
                  pl.BlockSpec((1, tn), lambda j: (0, j))],
        out_specs=pl.BlockSpec((8, tn), lambda j: (0, j)),
        out_shape=jax.ShapeDtypeStruct((8, n), F32),
        compiler_params=_cparams(("arbitrary",), 40),
        name="adaln",
    )(c8, w, b.reshape(1, n))
    return out[0:1]


def _norm_matmul_kernel(x_ref, g_ref, sc_ref, sh_ref, w_ref, o_ref, h_ref):
    @pl.when(pl.program_id(1) == 0)
    def _():
        x = x_ref[...]
        ms = jnp.mean(x * x, axis=-1, keepdims=True)
        y = x * lax.rsqrt(ms + NORM_EPS) * g_ref[...]
        h_ref[...] = (y * (1.0 + sc_ref[...]) + sh_ref[...]).astype(BF16)

    o_ref[...] = _dot(h_ref[...], w_ref[...]).astype(o_ref.dtype)


def _norm_matmul(x, g, sc, sh, w, out_dtype, tm, tn, name):
    s, d = x.shape
    n = w.shape[1]
    vec = pl.BlockSpec((1, d), lambda i, j: (0, 0))
    return pl.pallas_call(
        _norm_matmul_kernel,
        grid=(s // tm, n // tn),
        in_specs=[pl.BlockSpec((tm, d), lambda i, j: (i, 0)), vec, vec, vec,
                  pl.BlockSpec((d, tn), lambda i, j: (0, j))],
        out_specs=pl.BlockSpec((tm, tn), lambda i, j: (i, j)),
        out_shape=jax.ShapeDtypeStruct((s, n), out_dtype),
        scratch_shapes=[pltpu.VMEM((tm, d), BF16)],
        compiler_params=_cparams(("parallel", "arbitrary"), 56),
        name=name,
    )(x, g, sc, sh, w)


def _rope(x, a, bm, bp, half):
    n = x.shape[-1]
    return x * a + pltpu.roll(x, n - half, 1) * bm + pltpu.roll(x, half, 1) * bp


def _prep_kernel(q_ref, qi_ref, k_ref, kw_ref, ta_ref, ti_ref, kn_ref,
                 qo_ref, ko_ref, qio_ref, kio_ref, wio_ref):
    aa, abm, abp = ta_ref[0], ta_ref[1], ta_ref[2]
    ia, ibm, ibp = ti_ref[0], ti_ref[1], ti_ref[2]
    ah = ATT_ROPE_DIM // 2
    ih = IDX_ROPE_DIM // 2
    scale = ATT_HEAD_DIM ** -0.5 * LOG2_E
    for h in range(N_ATT_HEADS):
        sl = slice(h * ATT_HEAD_DIM, (h + 1) * ATT_HEAD_DIM)
        qo_ref[:, sl] = (_rope(q_ref[:, sl], aa, abm, abp, ah) * scale).astype(BF16)
    for h in range(N_KV_HEADS):
        sl = slice(h * ATT_HEAD_DIM, (h + 1) * ATT_HEAD_DIM)
        ko_ref[:, sl] = _rope(k_ref[:, sl], aa, abm, abp, ah).astype(BF16)
    for p in range(N_IDX_HEADS // 2):
        sl = slice(p * V7X_LANES, (p + 1) * V7X_LANES)
        y = _rope(qi_ref[:, sl], ia, ibm, ibp, ih).astype(BF16)
        qio_ref[2 * p] = y[:, :IDX_HEAD_DIM]
        qio_ref[2 * p + 1] = y[:, IDX_HEAD_DIM:]
    kw = kw_ref[...]
    lane = lax.broadcasted_iota(I32, kw.shape, 1)
    is_k = lane < IDX_HEAD_DIM
    mu = jnp.sum(jnp.where(is_k, kw, 0.0), axis=-1, keepdims=True) * (1.0 / IDX_HEAD_DIM)
    cen = kw - mu
    var = jnp.sum(jnp.where(is_k, cen * cen, 0.0), axis=-1, keepdims=True) * (1.0 / IDX_HEAD_DIM)
    kn = cen * lax.rsqrt(var + NORM_EPS) * kn_ref[0:1, :] + kn_ref[1:2, :]
    kn = _rope(kn, ia, ibm, ibp, ih)
    kio_ref[...] = kn[:, :IDX_HEAD_DIM].astype(BF16)
    wscale = N_IDX_HEADS ** -0.5 * IDX_HEAD_DIM ** -0.5
    wio_ref[...] = kw[:, IDX_HEAD_DIM:IDX_HEAD_DIM + N_IDX_HEADS] * wscale


def _rope_tables(pos, rot_dim, head_dim):
    half = rot_dim // 2
    inv_freq = jnp.power(ROPE_THETA, -jnp.arange(half, dtype=F32) * 2.0 / rot_dim)
    ang = pos.astype(F32)[:, None] * inv_freq
    cos, sin = jnp.cos(ang), jnp.sin(ang)
    s = pos.shape[0]
    rest = head_dim - rot_dim
    a = jnp.concatenate([cos, cos, jnp.ones((s, rest), F32)], axis=1)
    bm = jnp.concatenate([-sin, jnp.zeros((s, head_dim - half), F32)], axis=1)
    bp = jnp.concatenate([jnp.zeros((s, half), F32), sin, jnp.zeros((s, rest), F32)], axis=1)
    reps = V7X_LANES // head_dim
    return jnp.stack([jnp.tile(t, (1, reps)) for t in (a, bm, bp)])


def _dsa_prep(proj_f, pos, knorm_g, knorm_b, cols, tb):
    s = proj_f.shape[0]
    ta = _rope_tables(pos, ATT_ROPE_DIM, ATT_HEAD_DIM)
    ti = _rope_tables(pos, IDX_ROPE_DIM, IDX_HEAD_DIM)
    kn = jnp.zeros((2, V7X_LANES), F32)
    kn = kn.at[0, :IDX_HEAD_DIM].set(knorm_g).at[1, :IDX_HEAD_DIM].set(knorm_b)
    qw = N_ATT_HEADS * ATT_HEAD_DIM
    qiw = N_IDX_HEADS * IDX_HEAD_DIM
    kw = N_KV_HEADS * ATT_HEAD_DIM
    tab = pl.BlockSpec((3, tb, V7X_LANES), lambda i: (0, i, 0))
    return pl.pallas_call(
        _prep_kernel,
        grid=(s // tb,),
        in_specs=[pl.BlockSpec((tb, qw), lambda i: (i, cols["q"] // qw)),
                  pl.BlockSpec((tb, qiw), lambda i: (i, cols["qi"] // qiw)),
                  pl.BlockSpec((tb, kw), lambda i: (i, cols["k"] // kw)),
                  pl.BlockSpec((tb, V7X_LANES), lambda i: (i, cols["kiwi"] // V7X_LANES)),
                  tab, tab,
                  pl.BlockSpec((2, V7X_LANES), lambda i: (0, 0))],
        out_specs=[pl.BlockSpec((tb, qw), lambda i: (i, 0)),
                   pl.BlockSpec((tb, kw), lambda i: (i, 0)),
                   pl.BlockSpec((N_IDX_HEADS, tb, IDX_HEAD_DIM), lambda i: (0, i, 0)),
                   pl.BlockSpec((tb, IDX_HEAD_DIM), lambda i: (i, 0)),
                   pl.BlockSpec((tb, N_IDX_HEADS), lambda i: (i, 0))],
        out_shape=[jax.ShapeDtypeStruct((s, qw), BF16),
                   jax.ShapeDtypeStruct((s, kw), BF16),
                   jax.ShapeDtypeStruct((N_IDX_HEADS, s, IDX_HEAD_DIM), BF16),
                   jax.ShapeDtypeStruct((s, IDX_HEAD_DIM), BF16),
                   jax.ShapeDtypeStruct((s, N_IDX_HEADS), F32)],
        compiler_params=_cparams(("parallel",), 40),
        name="dsa_prep",
    )(proj_f, proj_f, proj_f, proj_f, ta, ti, kn)


def _order_key(x):
    b = lax.bitcast_convert_type(x, I32)
    return jnp.where(b < 0, b ^ jnp.int32(0x7FFFFFFF), b)


def _key_value(k):
    return lax.bitcast_convert_type(jnp.where(k < 0, k ^ jnp.int32(0x7FFFFFFF), k), F32)


KEY_NEG_INF = -2139095041
KEY_POS_INF = 2139095040
SEARCH_COLLECT_MAX = 16
SEARCH_INTERP_ITERS = 24
SOFTMAX_MIN_SUM = 1e-25


def _lane_chunk_sum(m, tk):
    acc = m[:, 0:V7X_LANES]
    for c in range(1, tk // V7X_LANES):
        acc = acc + m[:, c * V7X_LANES:(c + 1) * V7X_LANES]
    return acc


def _dsa_kernel(qi_ref, wi_ref, q_ref, ki_ref, k_ref, v_ref, o_ref,
                key_ref, acc_ref, s_ref, m_ref, l_ref, kmax_ref, *, tq, tk, topk):
    i = pl.program_id(0)
    n_kt = (i * tq + tq + tk - 1) // tk
    qpos = i * tq + lax.broadcasted_iota(I32, (tq, 1), 0)
    wi = wi_ref[...]
    rep = N_ATT_HEADS // N_KV_HEADS

    @pl.when(i == 0)
    def _():
        for g in range(N_KV_HEADS):
            gs = slice(g * ATT_HEAD_DIM, (g + 1) * ATT_HEAD_DIM)

            def body(t, m):
                r0 = pl.multiple_of(t * tk, tk)
                kk = k_ref[pl.ds(r0, tk), gs].astype(F32)
                n2 = jnp.sum(kk * kk, axis=-1, keepdims=True)
                return jnp.maximum(m, jnp.max(n2, axis=0, keepdims=True))

            m = lax.fori_loop(0, k_ref.shape[0] // tk, body, jnp.zeros((1, 1), F32))
            kmax_ref[g] = jnp.broadcast_to(jnp.sqrt(m), (1, V7X_LANES))

    def score_tile(kt, carry):
        m1, m2 = carry
        k0 = pl.multiple_of(kt * tk, tk)
        kit = ki_ref[pl.ds(k0, tk), :]
        acc = jnp.zeros((tq, tk), F32)
        for h in range(N_IDX_HEADS):
            d = _dot_nt(qi_ref[h], kit)
            acc = acc + wi[:, h:h + 1] * jnp.maximum(d, 0.0)
        kpos = k0 + lax.broadcasted_iota(I32, (1, tk), 1)
        key = jnp.where(kpos <= qpos, _order_key(acc), jnp.int32(INT_MIN))
        key_ref[kt] = key
        for c in range(tk // V7X_LANES):
            x = key[:, c * V7X_LANES:(c + 1) * V7X_LANES]
            m2 = jnp.maximum(m2, jnp.minimum(m1, x))
            m1 = jnp.maximum(m1, x)
        return m1, m2

    lowest = jnp.full((tq, V7X_LANES), INT_MIN, I32)
    m1, m2 = lax.fori_loop(0, n_kt, score_tile, (lowest, lowest))
    row_max = jnp.max(m1, axis=-1, keepdims=True)

    def count_ge(thr):
        def body(kt, acc):
            m = jnp.where(key_ref[kt] >= thr, 1, 0).astype(I32)
            return acc + _lane_chunk_sum(m, tk)
        acc = lax.fori_loop(0, n_kt, body, jnp.zeros((tq, V7X_LANES), I32))
        return jnp.sum(acc, axis=-1, keepdims=True)

    log_target = float(np.log(topk - 0.5))

    def any_row(mask):
        return jnp.max(jnp.where(mask, 1, 0)) > 0

    def wide_bracket(st):
        lo, hi, cnt_lo, cnt_hi = st[1:5]
        return any_row((lo + 1 < hi) & (cnt_lo - cnt_hi > SEARCH_COLLECT_MAX))

    def open_bracket(st):
        return any_row(st[1] + 1 < st[2])

    def search_body(st):
        it, lo, hi, cnt_lo, cnt_hi, f_lo, f_hi, side = st
        active = lo + 1 < hi
        vlo = _key_value(lo)
        vhi = _key_value(hi)
        frac = f_lo / (f_lo - f_hi)
        interp = _order_key(vlo + frac * (vhi - vlo))
        bisect = (lo & hi) + ((lo ^ hi) >> 1)
        use_interp = ((lo > KEY_NEG_INF) & (hi <= KEY_POS_INF)
                      & ((it < SEARCH_INTERP_ITERS) | ((it & 1) == 0)))
        mid = jnp.where(use_interp, interp, bisect)
        mid = jnp.minimum(jnp.maximum(mid, lo + 1), hi - 1)
        mid = jnp.where(active, mid, lo)
        cnt = count_ge(mid)
        up = active & (cnt >= topk)
        down = active & (cnt < topk)
        hi = jnp.where(up & (cnt == topk), mid + 1, jnp.where(down, mid, hi))
        lo = jnp.where(up, mid, lo)
        cnt_lo = jnp.where(up, cnt, cnt_lo)
        cnt_hi = jnp.where(down, cnt, cnt_hi)
        f_new = jnp.log(jnp.maximum(cnt.astype(F32), 0.5)) - log_target
        f_lo = jnp.where(up, f_new, jnp.where(down & (side < 0), 0.5 * f_lo, f_lo))
        f_hi = jnp.where(down, f_new, jnp.where(up & (side > 0), 0.5 * f_hi, f_hi))
        side = jnp.where(up, 1, jnp.where(down, -1, side))
        return it + 1, lo, hi, cnt_lo, cnt_hi, f_lo, f_hi, side

    lo0 = jnp.min(m2, axis=-1, keepdims=True)
    cnt0 = count_ge(lo0)
    hi0 = jnp.where(cnt0 == topk, lo0 + 1, row_max + 1)
    zero = jnp.zeros((tq, 1), I32)
    f_lo0 = jnp.log(cnt0.astype(F32)) - log_target
    f_hi0 = jnp.full((tq, 1), float(np.log(0.5)) - log_target, F32)
    st = lax.while_loop(wide_bracket, search_body, (zero, lo0, hi0, cnt0, zero, f_lo0, f_hi0, zero))
    it, lo, hi, cnt_lo, cnt_hi = st[0:5]

    def collect_tile(kt, carry):
        c1, c2, c3 = carry
        x = key_ref[kt]
        x = jnp.where(x >= lo, jnp.where(x < hi, x, jnp.int32(INT_MIN)), jnp.int32(INT_MIN))
        for c in range(tk // V7X_LANES):
            t = x[:, c * V7X_LANES:(c + 1) * V7X_LANES]
            c1, t = jnp.maximum(c1, t), jnp.minimum(c1, t)
            c2, t = jnp.maximum(c2, t), jnp.minimum(c2, t)
            c3 = jnp.maximum(c3, t)
        return c1, c2, c3

    cand = jnp.concatenate(lax.fori_loop(0, n_kt, collect_tile, (lowest, lowest, lowest)), axis=1)
    got = jnp.sum(jnp.where(cand > INT_MIN, 1, 0), axis=-1, keepdims=True)
    complete = got == cnt_lo - cnt_hi

    def small_cond(s):
        return any_row(complete & (s[0] + 1 < s[1]))

    def small_body(s):
        lo, hi, cnt_lo = s
        active = complete & (lo + 1 < hi)
        mid = jnp.where(active, (lo & hi) + ((lo ^ hi) >> 1), lo)
        cnt = cnt_hi + jnp.sum(jnp.where(cand >= mid, 1, 0), axis=-1, keepdims=True)
        up = active & (cnt >= topk)
        down = active & (cnt < topk)
        hi = jnp.where(up & (cnt == topk), mid + 1, jnp.where(down, mid, hi))
        return jnp.where(up, mid, lo), hi, jnp.where(up, cnt, cnt_lo)

    lo, hi, cnt_lo = lax.while_loop(small_cond, small_body, (lo, hi, cnt_lo))
    st = lax.while_loop(open_bracket, search_body, (it, lo, hi, cnt_lo) + tuple(st[4:]))
    thr, cnt_thr = st[1], st[3]

    tie_rows = (cnt_thr > topk) & (thr > INT_MIN)

    @pl.when(jnp.max(jnp.where(tie_rows, 1, 0)) > 0)
    def _():
        need = topk - count_ge(thr + 1)
        a = lax.broadcasted_iota(I32, (tk, tk), 0)
        b = lax.broadcasted_iota(I32, (tk, tk), 1)
        before = jnp.where(a < b, 1.0, 0.0).astype(BF16)

        def body(kt, seen):
            key = key_ref[kt]
            eq = (key == thr) & tie_rows
            eqf = jnp.where(eq, 1.0, 0.0)
            rank = seen + _dot(eqf.astype(BF16), before)
            drop = eq & (rank >= need.astype(F32))
            key_ref[kt] = jnp.where(drop, jnp.int32(INT_MIN), key)
            return seen + jnp.sum(eqf, axis=-1, keepdims=True)

        lax.fori_loop(0, n_kt, body, jnp.zeros((tq, 1), F32))

    thr = jnp.maximum(thr, INT_MIN + 1)

    acc_ref[...] = jnp.zeros(acc_ref.shape, F32)
    for h in range(N_ATT_HEADS):
        qh = q_ref[:, h * ATT_HEAD_DIM:(h + 1) * ATT_HEAD_DIM].astype(F32)
        qn = jnp.sqrt(jnp.sum(qh * qh, axis=-1, keepdims=True))
        m_ref[h] = qn * kmax_ref[h // rep][:, 0:1]
    ones_col = jnp.where(lax.broadcasted_iota(I32, (tk, V7X_LANES), 1) == 0, 1.0, 0.0).astype(BF16)

    def attend_tile(kt, carry):
        k0 = pl.multiple_of(kt * tk, tk)
        keep = jnp.where(key_ref[kt] >= thr, 1.0, 0.0)
        for h in range(N_ATT_HEADS):
            gs = slice((h // rep) * ATT_HEAD_DIM, (h // rep + 1) * ATT_HEAD_DIM)
            qh = q_ref[:, h * ATT_HEAD_DIM:(h + 1) * ATT_HEAD_DIM]
            s_ref[h] = _dot_nt(qh, k_ref[pl.ds(k0, tk), gs])
        for h in range(N_ATT_HEADS):
            gs = slice((h // rep) * ATT_HEAD_DIM, (h // rep + 1) * ATT_HEAD_DIM)
            p = jnp.exp2(s_ref[h] - m_ref[h]) * keep
            v_aug = jnp.concatenate([v_ref[pl.ds(k0, tk), gs], ones_col], axis=1)
            acc_ref[h] += _dot(p.astype(BF16), v_aug)
        return carry

    lax.fori_loop(0, n_kt, attend_tile, 0)
    underflow = jnp.int32(0)
    for h in range(N_ATT_HEADS):
        l = acc_ref[h, :, ATT_HEAD_DIM:ATT_HEAD_DIM + 1]
        underflow = jnp.maximum(underflow, jnp.max(jnp.where(l >= SOFTMAX_MIN_SUM, 0, 1)))
        o_ref[:, h * ATT_HEAD_DIM:(h + 1) * ATT_HEAD_DIM] = (acc_ref[h, :, 0:ATT_HEAD_DIM] / l).astype(o_ref.dtype)

    @pl.when(underflow > 0)
    def _():
        _attend_online(q_ref, k_ref, v_ref, o_ref, key_ref, acc_ref, m_ref, l_ref, thr, n_kt, tk)


def _attend_online(q_ref, k_ref, v_ref, o_ref, key_ref, acc_ref, m_ref, l_ref, thr, n_kt, tk):
    rep = N_ATT_HEADS // N_KV_HEADS
    d = ATT_HEAD_DIM
    m_ref[...] = jnp.full(m_ref.shape, MASKED_LOGIT, F32)
    l_ref[...] = jnp.zeros(l_ref.shape, F32)
    acc_ref[...] = jnp.zeros(acc_ref.shape, F32)

    def online_tile(kt, carry):
        k0 = pl.multiple_of(kt * tk, tk)
        sel = key_ref[kt] >= thr
        for g in range(N_KV_HEADS):
            kg = k_ref[pl.ds(k0, tk), g * d:(g + 1) * d]
            vg = v_ref[pl.ds(k0, tk), g * d:(g + 1) * d]
            for r in range(rep):
                h = g * rep + r
                s = jnp.where(sel, _dot_nt(q_ref[:, h * d:(h + 1) * d], kg), MASKED_LOGIT)
                m_old = m_ref[h]
                m_new = jnp.maximum(m_old, jnp.max(s, axis=-1, keepdims=True))
                alpha = jnp.exp2(m_old - m_new)
                p = jnp.exp2(s - m_new)
                l_ref[h] = alpha * l_ref[h] + jnp.sum(p, axis=-1, keepdims=True)
                acc_ref[h, :, 0:d] = alpha * acc_ref[h, :, 0:d] + _dot(p.astype(BF16), vg)
                m_ref[h] = m_new
        return carry

    lax.fori_loop(0, n_kt, online_tile, 0)
    for h in range(N_ATT_HEADS):
        o_ref[:, h * d:(h + 1) * d] = (acc_ref[h, :, 0:d] / l_ref[h]).astype(o_ref.dtype)


def _dsa(qi_hm, wi, q, ki, k, v_src, v_col, tq, tk, topk):
    s = q.shape[0]
    kw = N_KV_HEADS * ATT_HEAD_DIM
    qw = N_ATT_HEADS * ATT_HEAD_DIM
    whole = lambda shape, idx: pl.BlockSpec(shape, idx, pipeline_mode=pl.Buffered(1))
    kern = functools.partial(_dsa_kernel, tq=tq, tk=tk, topk=topk)
    return pl.pallas_call(
        kern,
        grid=(s // tq,),
        in_specs=[pl.BlockSpec((N_IDX_HEADS, tq, IDX_HEAD_DIM), lambda i: (0, i, 0)),
                  pl.BlockSpec((tq, N_IDX_HEADS), lambda i: (i, 0)),
                  pl.BlockSpec((tq, qw), lambda i: (i, 0)),
                  whole((s, IDX_HEAD_DIM), lambda i: (0, 0)),
                  whole((s, kw), lambda i: (0, 0)),
                  whole((s, kw), lambda i: (0, v_col // kw))],
        out_specs=pl.BlockSpec((tq, qw), lambda i: (i, 0)),
        out_shape=jax.ShapeDtypeStruct((s, qw), BF16),
        scratch_shapes=[pltpu.VMEM((s // tk, tq, tk), I32),
                        pltpu.VMEM((N_ATT_HEADS, tq, 2 * ATT_HEAD_DIM), F32),
                        pltpu.VMEM((N_ATT_HEADS, tq, tk), F32),
                        pltpu.VMEM((N_ATT_HEADS, tq, 1), F32),
                        pltpu.VMEM((N_ATT_HEADS, tq, 1), F32),
                        pltpu.VMEM((N_KV_HEADS, 1, V7X_LANES), F32)],
        compiler_params=_cparams(("arbitrary",), 56),
        name="dsa",
    )(qi_hm, wi, q, ki, k, v_src)


HGRN_BLOCK = 256
HGRN_LEVELS = (32, 64, 128, 256)


def _hgrn_constants():
    tb = HGRN_BLOCK
    t = np.arange(tb)[:, None]
    u = np.arange(tb)[None, :]
    lower = (u <= t).astype(np.float32)
    mats = [lower]
    c0 = HGRN_LEVELS[0]
    ref0 = c0 * (t // c0) + c0 // 2 - 1
    mats.append(lower - (u <= ref0))
    for c in HGRN_LEVELS[1:-1]:
        ref = c * (t // c) + c // 2 - 1
        mats.append(lower - (u <= ref))
    level = np.full((tb, tb), -1.0, np.float32)
    level[(t // c0 == u // c0) & (u <= t)] = 0.0
    for li, c in enumerate(HGRN_LEVELS[1:], start=1):
        sel = (t // c == u // c) & (t % c >= c // 2) & (u % c < c // 2)
        level[sel] = float(li)
    return np.concatenate(mats, axis=0), level


def _hgrn_kernel(q_ref, f_ref, i_ref, g_ref, lb_ref, gn_ref, mats_ref, lvl_ref, o_ref, st_ref):
    tb = HGRN_BLOCK
    d = HGRN_DIM

    @pl.when(pl.program_id(1) == 0)
    def _():
        st_ref[...] = jnp.zeros(st_ref.shape, F32)

    q = q_ref[...]
    qf = q * _sigmoid(q)
    lb = lb_ref[0]
    f = lb + (1.0 - lb) * _sigmoid(f_ref[...])
    logf = jnp.log(f)
    kf = 1.0 - f
    v = i_ref[...].astype(BF16)

    g1 = logf.astype(BF16)
    r1 = logf - g1.astype(F32)
    g2 = r1.astype(BF16)
    g3 = (r1 - g2.astype(F32)).astype(BF16)
    sums = _dot(mats_ref[...], jnp.concatenate([g1, g2, g3], axis=1))
    sums = sums[:, 0:d] + sums[:, d:2 * d] + sums[:, 2 * d:3 * d]
    n_mid = len(HGRN_LEVELS) - 1
    cum = sums[0:tb]
    rel = [sums[(l + 1) * tb:(l + 2) * tb] for l in range(n_mid)]
    rel.append(cum - cum[tb // 2 - 1:tb // 2, :])

    lvl = lvl_ref[...]
    amat = jnp.zeros((tb, tb), F32)
    for l, e in enumerate(rel):
        cap = EXP_CLAMP if l == 0 else 0.0
        qs = (qf * jnp.exp(jnp.minimum(e, cap))).astype(BF16)
        ks = (kf * jnp.exp(jnp.minimum(-e, cap))).astype(BF16)
        amat = jnp.where(lvl == float(l), _dot_nt(qs, ks), amat)

    st = st_ref[...]
    o = _dot(amat.astype(BF16), v) + _dot_nt((qf * jnp.exp(cum)).astype(BF16), st.astype(BF16))
    last = cum[tb - 1:tb, :]
    ks_state = (kf * jnp.exp(last - cum)).astype(BF16)
    st_ref[...] = st * jnp.exp(last) + _dot_tn(v, ks_state)

    ms = jnp.mean(o * o, axis=-1, keepdims=True)
    y = o * lax.rsqrt(ms + NORM_EPS) * gn_ref[...]
    gate = g_ref[...].astype(F32)
    o_ref[...] = (y * (gate * _sigmoid(gate))).astype(o_ref.dtype)


def _hgrn(proj_f, proj_b, lb, gn, cols_f, cols_b):
    s = proj_f.shape[0]
    tb = HGRN_BLOCK
    d = HGRN_DIM
    mats_np, lvl_np = _hgrn_constants()
    mats = jnp.asarray(mats_np, BF16)
    lvl = jnp.asarray(lvl_np, F32)
    col = lambda off: (lambda h, j: (j, off // d + h))
    const = lambda shape: pl.BlockSpec(shape, lambda h, j: (0,) * len(shape))
    return pl.pallas_call(
        _hgrn_kernel,
        grid=(N_HGRN_HEADS, s // tb),
        in_specs=[pl.BlockSpec((tb, d), col(cols_f["hq"])),
                  pl.BlockSpec((tb, d), col(cols_f["hf"])),
                  pl.BlockSpec((tb, d), col(cols_f["hi"])),
                  pl.BlockSpec((tb, d), col(cols_b["hg"])),
                  pl.BlockSpec((1, 1, d), lambda h, j: (h, 0, 0)),
                  const((1, d)),
                  const(mats_np.shape),
                  const(lvl_np.shape)],
        out_specs=pl.BlockSpec((tb, d), lambda h, j: (j, h)),
        out_shape=jax.ShapeDtypeStruct((s, N_HGRN_HEADS * d), BF16),
        scratch_shapes=[pltpu.VMEM((d, d), F32)],
        compiler_params=_cparams(("parallel", "arbitrary"), 32),
        name="hgrn",
    )(proj_f, proj_f, proj_f, proj_b, lb.reshape(N_HGRN_HEADS, 1, d), gn.reshape(1, d), mats, lvl)


def _post_kernel(att_ref, rec_ref, ga_ref, gr_ref, x_ref, wpa_ref, wph_ref, wo_ref,
                 gt_ref, g2_ref, sc_ref, sh_ref, wr_ref, br_ref,
                 x1_ref, h2_ref, te_ref, tg_ref, sel_ref):
    a = _dot(att_ref[...], wpa_ref[...])
    r = _dot(rec_ref[...], wph_ref[...])
    merged = _sigmoid(ga_ref[...].astype(F32)) * a + _sigmoid(gr_ref[...].astype(F32)) * r
    mix = _dot(merged.astype(BF16), wo_ref[...])
    x1 = x_ref[...] + gt_ref[...] * mix
    x1_ref[...] = x1
    ms = jnp.mean(x1 * x1, axis=-1, keepdims=True)
    h2 = x1 * lax.rsqrt(ms + NORM_EPS) * g2_ref[...]
    h2 = h2 * (1.0 + sc_ref[...]) + sh_ref[...]
    h2_ref[...] = _pack_rows(h2)
    logits = jnp.dot(h2, wr_ref[...], preferred_element_type=F32,
                     precision=lax.Precision.HIGHEST) + br_ref[...]
    ne = logits.shape[-1]
    lane = lax.broadcasted_iota(I32, logits.shape, 1)
    work = logits
    vals, idxs = [], []
    sel = jnp.zeros(logits.shape, F32)
    for _ in range(MOE_TOPK):
        m = jnp.max(work, axis=-1, keepdims=True)
        idx = jnp.min(jnp.where(work == m, lane, ne), axis=-1, keepdims=True)
        hit = lane == idx
        vals.append(m)
        idxs.append(idx)
        sel = jnp.where(hit, 1.0, sel)
        work = jnp.where(hit, -jnp.inf, work)
    es = [jnp.exp(vk - vals[0]) for vk in vals]
    denom = es[0]
    for e in es[1:]:
        denom = denom + e
    lane4 = lax.broadcasted_iota(I32, te_ref.shape, 1)
    te = jnp.zeros(te_ref.shape, I32)
    tg = jnp.zeros(tg_ref.shape, F32)
    for kk in range(MOE_TOPK):
        te = jnp.where(lane4 == kk, idxs[kk], te)
        tg = jnp.where(lane4 == kk, es[kk] / denom, tg)
    te_ref[...] = te
    tg_ref[...] = tg
    sel_ref[...] = sel.astype(sel_ref.dtype)


def _post_mixer(att, rec, proj_b, gates_col, x, wpa, wph, wo, gt1, g2, sc2, sh2, wr, br, tm):
    s, d = x.shape
    aw = att.shape[1]
    rw = rec.shape[1]
    ne = wr.shape[1]
    const = lambda shape: pl.BlockSpec(shape, lambda i: (0,) * len(shape), pipeline_mode=pl.Buffered(1))
    row = lambda w: pl.BlockSpec((tm, w), lambda i: (i, 0))
    gcol = gates_col // d
    return pl.pallas_call(
        _post_kernel,
        grid=(s // tm,),
        in_specs=[row(aw), row(rw),
                  pl.BlockSpec((tm, d), lambda i: (i, gcol)),
                  pl.BlockSpec((tm, d), lambda i: (i, gcol + 1)),
                  row(d),
                  const((aw, d)), const((rw, d)), const((d, d)),
                  const((1, d)), const((1, d)), const((1, d)), const((1, d)),
                  const((d, ne)), const((1, ne))],
        out_specs=[row(d), row(d // 2), row(MOE_TOPK), row(MOE_TOPK), row(ne)],
        out_shape=[jax.ShapeDtypeStruct((s, d), F32),
                   jax.ShapeDtypeStruct((s, d // 2), I32),
                   jax.ShapeDtypeStruct((s, MOE_TOPK), I32),
                   jax.ShapeDtypeStruct((s, MOE_TOPK), F32),
                   jax.ShapeDtypeStruct((s, ne), BF16)],
        compiler_params=_cparams(("parallel",), 56),
        name="post_mixer",
    )(att, rec, proj_b, proj_b, x, wpa, wph, wo, gt1, g2, sc2, sh2, wr, br)


def _rank_kernel(sel_ref, rank_ref, cnt_ref, carry_ref):
    @pl.when(pl.program_id(0) == 0)
    def _():
        carry_ref[...] = jnp.zeros(carry_ref.shape, F32)

    sel = sel_ref[...]
    tt = sel.shape[0]
    a = lax.broadcasted_iota(I32, (tt, tt), 0)
    b = lax.broadcasted_iota(I32, (tt, tt), 1)
    before = jnp.where(b < a, 1.0, 0.0).astype(BF16)
    carry = carry_ref[...]
    rank_ref[...] = _dot(before, sel) + carry
    carry = carry + jnp.sum(sel.astype(F32), axis=0, keepdims=True)
    carry_ref[...] = carry
    cnt_ref[...] = carry


def _expert_ranks(sel, tt):
    s, ne = sel.shape
    return pl.pallas_call(
        _rank_kernel,
        grid=(s // tt,),
        in_specs=[pl.BlockSpec((tt, ne), lambda i: (i, 0))],
        out_specs=[pl.BlockSpec((tt, ne), lambda i: (i, 0)),
                   pl.BlockSpec((1, ne), lambda i: (0, 0))],
        out_shape=[jax.ShapeDtypeStruct((s, ne), F32),
                   jax.ShapeDtypeStruct((1, ne), F32)],
        scratch_shapes=[pltpu.VMEM((1, ne), F32)],
        compiler_params=_cparams(("arbitrary",), 16),
        name="expert_ranks",
    )(sel)


def _row_copy(src_ref, src_row, dst_ref, dst_row, sem):
    return pltpu.make_async_copy(src_ref.at[pl.ds(src_row, 1)], dst_ref.at[pl.ds(dst_row, 1)], sem)


def _scatter_kernel(dest_ref, h_ref, xb_in_ref, xb_ref, sem, *, tt):
    del xb_in_ref
    base = pl.program_id(0) * tt * MOE_TOPK

    def issue(t, c):
        for kk in range(MOE_TOPK):
            _row_copy(h_ref, t, xb_ref, dest_ref[base + t * MOE_TOPK + kk], sem).start()
        return c

    lax.fori_loop(0, tt, issue, 0)

    def drain(t, c):
        for kk in range(MOE_TOPK):
            _row_copy(h_ref, 0, xb_ref, 0, sem).wait()
        return c

    lax.fori_loop(0, tt, drain, 0)


def _scatter_rows(dest_flat, h2, n_rows, tt):
    s, d = h2.shape
    xb0 = jnp.zeros((n_rows, d), h2.dtype)
    grid_spec = pltpu.PrefetchScalarGridSpec(
        num_scalar_prefetch=1,
        grid=(s // tt,),
        in_specs=[pl.BlockSpec((tt, d), lambda i, dest: (i, 0)),
                  pl.BlockSpec(memory_space=pl.ANY)],
        out_specs=pl.BlockSpec(memory_space=pl.ANY),
        scratch_shapes=[pltpu.SemaphoreType.DMA(())],
    )
    return pl.pallas_call(
        functools.partial(_scatter_kernel, tt=tt),
        grid_spec=grid_spec,
        out_shape=jax.ShapeDtypeStruct((n_rows, d), h2.dtype),
        input_output_aliases={2: 0},
        compiler_params=_cparams(("arbitrary",), 32),
        name="moe_scatter",
    )(dest_flat, h2, xb0)


ITEM_UNUSED, ITEM_USED, ITEM_FIRST = 0, 1, 3


def _unpack_rows(words):
    lo = lax.bitcast_convert_type(words << 16, F32)
    hi = lax.bitcast_convert_type(words & jnp.int32(-65536), F32)
    return jnp.concatenate([lo.astype(BF16), hi.astype(BF16)], axis=1)


def _pack_rows(x):
    half = x.shape[1] // 2
    lo = lax.bitcast_convert_type(x[:, :half].astype(BF16).astype(F32), I32)
    hi = lax.bitcast_convert_type(x[:, half:].astype(BF16).astype(F32), I32)
    return lax.shift_right_logical(lo, jnp.int32(16)) | (hi & jnp.int32(-65536))


def _expert_up_kernel(sb_ref, wc_ref, oc_ref, e_ref, flag_ref, val_ref,
                      x_ref, w1g_ref, w1l_ref, b1g_ref, b1l_ref, a_ref, wg_ref, wl_ref):
    i = pl.program_id(0)
    flag = flag_ref[i]

    @pl.when(flag == ITEM_FIRST)
    def _():
        wg_ref[...] = w1g_ref[0].astype(BF16)
        wl_ref[...] = w1l_ref[0].astype(BF16)

    @pl.when(flag != ITEM_UNUSED)
    def _():
        rows = lax.broadcasted_iota(I32, (x_ref.shape[0], 1), 0)
        x = _unpack_rows(jnp.where(rows < val_ref[i], x_ref[...], 0))
        ug = _dot(x, wg_ref[...]) + b1g_ref[0]
        ul = _dot(x, wl_ref[...]) + b1l_ref[0]
        ug = jnp.minimum(ug, SWIGLU_LIMIT)
        ul = jnp.clip(ul, -SWIGLU_LIMIT, SWIGLU_LIMIT)
        a_ref[...] = (ug * _sigmoid(SWIGLU_ALPHA * ug) * (ul + 1.0)).astype(a_ref.dtype)

    @pl.when(flag == ITEM_UNUSED)
    def _():
        a_ref[...] = jnp.zeros(a_ref.shape, a_ref.dtype)


def _expert_down_kernel(sb_ref, wc_ref, oc_ref, e_ref, flag_ref, val_ref,
                        a_ref, w2_ref, b2_ref, y_ref, w_ref):
    i = pl.program_id(0)
    flag = flag_ref[i]

    @pl.when(flag == ITEM_FIRST)
    def _():
        w_ref[...] = w2_ref[0].astype(BF16)

    @pl.when(flag != ITEM_UNUSED)
    def _():
        y_ref[...] = _dot(a_ref[...], w_ref[...]) + b2_ref[0]

    @pl.when(flag == ITEM_UNUSED)
    def _():
        y_ref[...] = jnp.zeros(y_ref.shape, y_ref.dtype)


def _expert_items(sb_expert, sb_valid, n_used, pstart, pcnt, n_sb, rb, nch):
    idx = jnp.arange(n_sb * nch, dtype=I32)
    nu = n_used[0]
    used = idx < nu * nch
    e = sb_expert[jnp.minimum(idx // nch, nu - 1)]
    first_sb = pstart[e] // rb
    n_blk = jnp.maximum(pcnt[e] // rb, 1)
    local = idx - nch * first_sb
    n_free = jnp.maximum(n_sb - nu, 1)
    spare = idx - nch * nu
    item_sb = jnp.where(used, first_sb + local % n_blk, nu + spare % n_free)
    chunk = jnp.where(used, local // n_blk, spare // n_free)
    item_wc = jnp.where(used, chunk, nch - 1)
    flag = jnp.where(used, jnp.where(local % n_blk == 0, ITEM_FIRST, ITEM_USED), ITEM_UNUSED)
    return (item_sb.astype(I32), item_wc.astype(I32), chunk.astype(I32), e.astype(I32),
            flag.astype(I32), sb_valid[item_sb].astype(I32))


def _experts(items, xb, w1, b1, w2, b2, rb, th):
    n_rows = xb.shape[0]
    ne, d, two_de = w1.shape
    de = two_de // 2
    nch = de // th
    n_items = items[0].shape[0]
    b1r = b1.reshape(ne, 1, two_de)
    up_spec = pltpu.PrefetchScalarGridSpec(
        num_scalar_prefetch=6,
        grid=(n_items,),
        in_specs=[pl.BlockSpec((rb, d // 2), lambda i, sb, wc, oc, e, f, v: (sb[i], 0)),
                  pl.BlockSpec((1, d, th), lambda i, sb, wc, oc, e, f, v: (e[i], 0, wc[i])),
                  pl.BlockSpec((1, d, th), lambda i, sb, wc, oc, e, f, v: (e[i], 0, nch + wc[i])),
                  pl.BlockSpec((1, 1, th), lambda i, sb, wc, oc, e, f, v: (e[i], 0, wc[i])),
                  pl.BlockSpec((1, 1, th), lambda i, sb, wc, oc, e, f, v: (e[i], 0, nch + wc[i]))],
        out_specs=pl.BlockSpec((rb, th), lambda i, sb, wc, oc, e, f, v: (sb[i], oc[i])),
        scratch_shapes=[pltpu.VMEM((d, th), BF16), pltpu.VMEM((d, th), BF16)],
    )
    act = pl.pallas_call(
        _expert_up_kernel,
        grid_spec=up_spec,
        out_shape=jax.ShapeDtypeStruct((n_rows, de), BF16),
        compiler_params=_cparams(("arbitrary",), 58),
        name="moe_expert_up",
    )(*items, xb, w1, w1, b1r, b1r)
    tn = d // nch
    down_spec = pltpu.PrefetchScalarGridSpec(
        num_scalar_prefetch=6,
        grid=(n_items,),
        in_specs=[pl.BlockSpec((rb, de), lambda i, sb, wc, oc, e, f, v: (sb[i], 0)),
                  pl.BlockSpec((1, de, tn), lambda i, sb, wc, oc, e, f, v: (e[i], 0, wc[i])),
                  pl.BlockSpec((1, 1, tn), lambda i, sb, wc, oc, e, f, v: (e[i], 0, wc[i]))],
        out_specs=pl.BlockSpec((rb, tn), lambda i, sb, wc, oc, e, f, v: (sb[i], oc[i])),
        scratch_shapes=[pltpu.VMEM((de, tn), BF16)],
    )
    return pl.pallas_call(
        _expert_down_kernel,
        grid_spec=down_spec,
        out_shape=jax.ShapeDtypeStruct((n_rows, d), F32),
        compiler_params=_cparams(("arbitrary",), 48),
        name="moe_expert_down",
    )(*items, act, w2, b2.reshape(ne, 1, d))


def _combine_kernel(dest_ref, yb_ref, x1_ref, tg_ref, gt_ref, gn_ref, o_ref, buf_ref, sem, *, tt):
    base = pl.program_id(0) * tt * MOE_TOPK

    def issue(t, c):
        for kk in range(MOE_TOPK):
            _row_copy(yb_ref, dest_ref[base + t * MOE_TOPK + kk], buf_ref.at[kk], t, sem).start()
        return c

    lax.fori_loop(0, tt, issue, 0)

    def drain(t, c):
        for kk in range(MOE_TOPK):
            _row_copy(yb_ref, 0, buf_ref.at[kk], 0, sem).wait()
        return c

    lax.fori_loop(0, tt, drain, 0)

    tg = tg_ref[...]
    moe = tg[:, 0:1] * buf_ref[0]
    for kk in range(1, MOE_TOPK):
        moe = moe + tg[:, kk:kk + 1] * buf_ref[kk]
    x2 = x1_ref[...] + gt_ref[...] * moe
    ms = jnp.mean(x2 * x2, axis=-1, keepdims=True)
    o_ref[...] = x2 * lax.rsqrt(ms + NORM_EPS) * gn_ref[...]


def _combine(dest_flat, yb, x1, tg, gt2, gn, tt):
    s, d = x1.shape
    grid_spec = pltpu.PrefetchScalarGridSpec(
        num_scalar_prefetch=1,
        grid=(s // tt,),
        in_specs=[pl.BlockSpec(memory_space=pl.ANY),
                  pl.BlockSpec((tt, d), lambda i, dest: (i, 0)),
                  pl.BlockSpec((tt, MOE_TOPK), lambda i, dest: (i, 0)),
                  pl.BlockSpec((1, d), lambda i, dest: (0, 0)),
                  pl.BlockSpec((1, d), lambda i, dest: (0, 0))],
        out_specs=pl.BlockSpec((tt, d), lambda i, dest: (i, 0)),
        scratch_shapes=[pltpu.VMEM((MOE_TOPK, tt, d), F32),
                        pltpu.SemaphoreType.DMA(())],
    )
    return pl.pallas_call(
        functools.partial(_combine_kernel, tt=tt),
        grid_spec=grid_spec,
        out_shape=jax.ShapeDtypeStruct((s, d), F32),
        compiler_params=_cparams(("arbitrary",), 40),
        name="moe_combine",
    )(dest_flat, yb, x1, tg, gt2, gn)


MOE_ROW_BLOCK = 512
MOE_HIDDEN_TILE = 1024


def _pad_cols(w, n):
    return jnp.pad(w, ((0, 0), (0, n - w.shape[1])))


def kernel(x, c, positions, w_ada, b_ada, norm1_g, w_in, idx_knorm_g, idx_knorm_b, hgrn_lb_logits,
           hgrn_norm_g, w_proj_att, w_proj_hgrn, w_out, norm2_g, w_router, b_router, w1, b1, w2, b2,
           final_norm_g):
    bsz, s, d = x.shape
    assert bsz == 1 and w_ada.shape[0] == 1
    xs = x.reshape(s, d)
    pos = positions.reshape(s)

    mod = _adaln(c, w_ada[0], b_ada[0])
    sh1, sc1, gt1, sh2, sc2, gt2 = [mod[:, i * d:(i + 1) * d] for i in range(6)]

    widths = dict(q=N_ATT_HEADS * ATT_HEAD_DIM, k=N_KV_HEADS * ATT_HEAD_DIM, v=N_KV_HEADS * ATT_HEAD_DIM,
                  qi=N_IDX_HEADS * IDX_HEAD_DIM, ki=IDX_HEAD_DIM, wi=N_IDX_HEADS,
                  hq=N_HGRN_HEADS * HGRN_DIM, hf=N_HGRN_HEADS * HGRN_DIM, hi=N_HGRN_HEADS * HGRN_DIM,
                  hg=N_HGRN_HEADS * HGRN_DIM, gates=2 * d)
    src, off = {}, 0
    for name in ("q", "k", "v", "qi", "ki", "wi", "hq", "hf", "hi", "hg", "gates"):
        src[name] = (off, widths[name])
        off += widths[name]
    w0 = w_in[0]
    take = lambda name: lax.slice_in_dim(w0, src[name][0], src[name][0] + src[name][1], axis=1)
    kiwi = _pad_cols(jnp.concatenate([take("ki"), take("wi")], axis=1), V7X_LANES)
    f_names = ("q", "qi", "hq", "hf", "hi", "k")
    f_parts = [take(n) for n in f_names] + [kiwi]
    cols_f, o = {}, 0
    for n, p in zip(f_names + ("kiwi",), f_parts):
        cols_f[n] = o
        o += p.shape[1]
    tn_f = 2816
    w_f = _pad_cols(jnp.concatenate(f_parts, axis=1), -(-o // tn_f) * tn_f).astype(BF16)
    b_names = ("gates", "hg", "v")
    cols_b, o = {}, 0
    for n in b_names:
        cols_b[n] = o
        o += widths[n]
    w_b = jnp.concatenate([take(n) for n in b_names], axis=1).astype(BF16)

    g1 = norm1_g[0].reshape(1, d)
    proj_f = _norm_matmul(xs, g1, sc1, sh1, w_f, F32, 512, tn_f, "in_proj_f32")
    proj_b = _norm_matmul(xs, g1, sc1, sh1, w_b, BF16, 512, 1792, "in_proj_bf16")

    q_r, k_r, qi_hm, ki_r, wi_r = _dsa_prep(proj_f, pos, idx_knorm_g[0], idx_knorm_b[0], cols_f, 512)
    topk = min(IDX_TOPK, s // 4)
    att = _dsa(qi_hm, wi_r, q_r, ki_r, k_r, proj_b, cols_b["v"], 128, 1024, topk)

    lb_all = jnp.cumsum(jax.nn.softmax(hgrn_lb_logits.astype(F32), axis=0), axis=0)
    rec = _hgrn(proj_f, proj_b, lb_all[0], hgrn_norm_g[0], cols_f, cols_b)

    x1, h2, top_e, top_g, sel = _post_mixer(
        att, rec, proj_b, cols_b["gates"], xs,
        w_proj_att[0].astype(BF16), w_proj_hgrn[0].astype(BF16), w_out[0].astype(BF16),
        gt1, norm2_g[0].reshape(1, d), sc2, sh2, w_router[0], b_router[0].reshape(1, -1), 256)

    rank, counts = _expert_ranks(sel, 512)
    rb = MOE_ROW_BLOCK
    n_sb = s * MOE_TOPK // rb + N_EXPERTS
    cnt = counts[0].astype(I32)
    pcnt = (cnt + rb - 1) // rb * rb
    pend = jnp.cumsum(pcnt)
    pstart = pend - pcnt
    dest = jnp.take_along_axis(rank.astype(I32) + pstart[None, :], top_e, axis=1).reshape(-1)
    sb_row = jnp.arange(n_sb, dtype=I32) * rb
    sb_expert = jnp.minimum(jnp.sum((pend[None, :] <= sb_row[:, None]).astype(I32), axis=1), N_EXPERTS - 1)
    sb_valid = jnp.clip(cnt[sb_expert] - (sb_row - pstart[sb_expert]), 0, rb).astype(I32)
    n_used = (pend[-1] // rb).astype(I32).reshape(1)

    xb = _scatter_rows(dest, h2, n_sb * rb, 512)
    items = _expert_items(sb_expert, sb_valid, n_used, pstart, pcnt, n_sb, rb, w2.shape[2] // MOE_HIDDEN_TILE)
    yb = _experts(items, xb, w1[0], b1[0], w2[0], b2[0], rb, MOE_HIDDEN_TILE)
    out = _combine(dest, yb, x1, top_g, gt2, final_norm_g.reshape(1, d), 256)
    return out.reshape(bsz, s, d)
```

```python
import functools

import numpy as np
import jax
import jax.numpy as jnp
from jax import lax
from jax.experimental import pallas as pl
from jax.experimental.pallas import tpu as pltpu

F32 = jnp.float32
BF16 = jnp.bfloat16
I32 = jnp.int32

ATT_HEAD_DIM = 128
N_ATT_HEADS = 8
N_KV_HEADS = 2
ATT_ROPE_DIM = 32
IDX_HEAD_DIM = 64
N_IDX_HEADS = 16
IDX_ROPE_DIM = 16
IDX_TOPK = 256
ROPE_THETA = 500000.0
HGRN_DIM = 128
N_HGRN_HEADS = 8
N_EXPERTS = 32
MOE_TOPK = 4
SWIGLU_LIMIT = 7.0
SWIGLU_ALPHA = 1.702
NORM_EPS = 1e-5

V7X_LANES = 128
INT_MIN = -2147483648
MASKED_LOGIT = -1e30
EXP_CLAMP = 80.0
LOG2_E = 1.4426950408889634

MIB = 1024 * 1024


def _cparams(sem, vmem_mib):
    return pltpu.CompilerParams(dimension_semantics=sem, vmem_limit_bytes=vmem_mib * MIB)


def _sigmoid(x):
    return 1.0 / (1.0 + jnp.exp(-x))


def _dot(a, b):
    return jnp.dot(a, b, preferred_element_type=F32)


def _dot_nt(a, b):
    return lax.dot_general(a, b, (((1,), (1,)), ((), ())), preferred_element_type=F32)


def _dot_tn(a, b):
    return lax.dot_general(a, b, (((0,), (0,)), ((), ())), preferred_element_type=F32)


def _adaln_kernel(c_ref, w_ref, b_ref, o_ref):
    c = c_ref[...]
    cs = c * _sigmoid(c)
    o_ref[...] = jnp.dot(cs, w_ref[...], preferred_element_type=F32,
                         precision=lax.Precision.HIGHEST) + b_ref[...]


def _adaln(c, w, b):
    d, n = w.shape
    tn = 1024
    c8 = jnp.broadcast_to(c, (8, d))
    out = pl.pallas_call(
        _adaln_kernel,
        grid=(n // tn,),
        in_specs=[pl.BlockSpec((8, d), lambda j: (0, 0)),
                  pl.BlockSpec((d, tn), lambda j: (0, j)),
                  pl.BlockSpec((1, tn), lambda j: (0, j))],
        out_specs=pl.BlockSpec((8, tn), lambda j: (0, j)),
        out_shape=jax.ShapeDtypeStruct((8, n), F32),
        compiler_params=_cparams(("arbitrary",), 40),
        name="adaln",
    )(c8, w, b.reshape(1, n))
    return out[0:1]


def _norm_matmul_kernel(x_ref, g_ref, sc_ref, sh_ref, w_ref, o_ref, h_ref):
    @pl.when(pl.program_id(1) == 0)
    def _():
        x = x_ref[...]
        ms = jnp.mean(x * x, axis=-1, keepdims=True)
        y = x * lax.rsqrt(ms + NORM_EPS) * g_ref[...]
        h_ref[...] = (y * (1.0 + sc_ref[...]) + sh_ref[...]).astype(BF16)

    o_ref[...] = _dot(h_ref[...], w_ref[...]).astype(o_ref.dtype)


def _norm_matmul(x, g, sc, sh, w, out_dtype, tm, tn, name):
    s, d = x.shape
    n = w.shape[1]
    vec = pl.BlockSpec((1, d), lambda i, j: (0, 0))
    return pl.pallas_call(
        _norm_matmul_kernel,
        grid=(s // tm, n // tn),
        in_specs=[pl.BlockSpec((tm, d), lambda i, j: (i, 0)), vec, vec, vec,
                  pl.BlockSpec((d, tn), lambda i, j: (0, j))],
        out_specs=pl.BlockSpec((tm, tn), lambda i, j: (i, j)),
        out_shape=jax.ShapeDtypeStruct((s, n), out_dtype),
        scratch_shapes=[pltpu.VMEM((tm, d), BF16)],
        compiler_params=_cparams(("parallel", "arbitrary"), 56),
        name=name,
    )(x, g, sc, sh, w)


def _rope(x, a, bm, bp, half):
    n = x.shape[-1]
    return x * a + pltpu.roll(x, n - half, 1) * bm + pltpu.roll(x, half, 1) * bp


def _prep_kernel(q_ref, qi_ref, k_ref, kw_ref, ta_ref, ti_ref, kn_ref,
                 qo_ref, ko_ref, qio_ref, kio_ref, wio_ref):
    aa, abm, abp = ta_ref[0], ta_ref[1], ta_ref[2]
    ia, ibm, ibp = ti_ref[0], ti_ref[1], ti_ref[2]
    ah = ATT_ROPE_DIM // 2
    ih = IDX_ROPE_DIM // 2
    scale = ATT_HEAD_DIM ** -0.5 * LOG2_E
    for h in range(N_ATT_HEADS):
        sl = slice(h * ATT_HEAD_DIM, (h + 1) * ATT_HEAD_DIM)
        qo_ref[:, sl] = (_rope(q_ref[:, sl], aa, abm, abp, ah) * scale).astype(BF16)
    for h in range(N_KV_HEADS):
        sl = slice(h * ATT_HEAD_DIM, (h + 1) * ATT_HEAD_DIM)
        ko_ref[:, sl] = _rope(k_ref[:, sl], aa, abm, abp, ah).astype(BF16)
    for p in range(N_IDX_HEADS // 2):
        sl = slice(p * V7X_LANES, (p + 1) * V7X_LANES)
        y = _rope(qi_ref[:, sl], ia, ibm, ibp, ih).astype(BF16)
        qio_ref[2 * p] = y[:, :IDX_HEAD_DIM]
        qio_ref[2 * p + 1] = y[:, IDX_HEAD_DIM:]
    kw = kw_ref[...]
    lane = lax.broadcasted_iota(I32, kw.shape, 1)
    is_k = lane < IDX_HEAD_DIM
    mu = jnp.sum(jnp.where(is_k, kw, 0.0), axis=-1, keepdims=True) * (1.0 / IDX_HEAD_DIM)
    cen = kw - mu
    var = jnp.sum(jnp.where(is_k, cen * cen, 0.0), axis=-1, keepdims=True) * (1.0 / IDX_HEAD_DIM)
    kn = cen * lax.rsqrt(var + NORM_EPS) * kn_ref[0:1, :] + kn_ref[1:2, :]
    kn = _rope(kn, ia, ibm, ibp, ih)
    kio_ref[...] = kn[:, :IDX_HEAD_DIM].astype(BF16)
    wscale = N_IDX_HEADS ** -0.5 * IDX_HEAD_DIM ** -0.5
    wio_ref[...] = kw[:, IDX_HEAD_DIM:IDX_HEAD_DIM + N_IDX_HEADS] * wscale


def _rope_tables(pos, rot_dim, head_dim):
    half = rot_dim // 2
    inv_freq = jnp.power(ROPE_THETA, -jnp.arange(half, dtype=F32) * 2.0 / rot_dim)
    ang = pos.astype(F32)[:, None] * inv_freq
    cos, sin = jnp.cos(ang), jnp.sin(ang)
    s = pos.shape[0]
    rest = head_dim - rot_dim
    a = jnp.concatenate([cos, cos, jnp.ones((s, rest), F32)], axis=1)
    bm = jnp.concatenate([-sin, jnp.zeros((s, head_dim - half), F32)], axis=1)
    bp = jnp.concatenate([jnp.zeros((s, half), F32), sin, jnp.zeros((s, rest), F32)], axis=1)
    reps = V7X_LANES // head_dim
    return jnp.stack([jnp.tile(t, (1, reps)) for t in (a, bm, bp)])


def _dsa_prep(proj_f, pos, knorm_g, knorm_b, cols, tb):
    s = proj_f.shape[0]
    ta = _rope_tables(pos, ATT_ROPE_DIM, ATT_HEAD_DIM)
    ti = _rope_tables(pos, IDX_ROPE_DIM, IDX_HEAD_DIM)
    kn = jnp.zeros((2, V7X_LANES), F32)
    kn = kn.at[0, :IDX_HEAD_DIM].set(knorm_g).at[1, :IDX_HEAD_DIM].set(knorm_b)
    qw = N_ATT_HEADS * ATT_HEAD_DIM
    qiw = N_IDX_HEADS * IDX_HEAD_DIM
    kw = N_KV_HEADS * ATT_HEAD_DIM
    tab = pl.BlockSpec((3, tb, V7X_LANES), lambda i: (0, i, 0))
    return pl.pallas_call(
        _prep_kernel,
        grid=(s // tb,),
        in_specs=[pl.BlockSpec((tb, qw), lambda i: (i, cols["q"] // qw)),
                  pl.BlockSpec((tb, qiw), lambda i: (i, cols["qi"] // qiw)),
                  pl.BlockSpec((tb, kw), lambda i: (i, cols["k"] // kw)),
                  pl.BlockSpec((tb, V7X_LANES), lambda i: (i, cols["kiwi"] // V7X_LANES)),
                  tab, tab,
                  pl.BlockSpec((2, V7X_LANES), lambda i: (0, 0))],
        out_specs=[pl.BlockSpec((tb, qw), lambda i: (i, 0)),
                   pl.BlockSpec((tb, kw), lambda i: (i, 0)),
                   pl.BlockSpec((N_IDX_HEADS, tb, IDX_HEAD_DIM), lambda i: (0, i, 0)),
                   pl.BlockSpec((tb, IDX_HEAD_DIM), lambda i: (i, 0)),
                   pl.BlockSpec((tb, N_IDX_HEADS), lambda i: (i, 0))],
        out_shape=[jax.ShapeDtypeStruct((s, qw), BF16),
                   jax.ShapeDtypeStruct((s, kw), BF16),
                   jax.ShapeDtypeStruct((N_IDX_HEADS, s, IDX_HEAD_DIM), BF16),
                   jax.ShapeDtypeStruct((s, IDX_HEAD_DIM), BF16),
                   jax.ShapeDtypeStruct((s, N_IDX_HEADS), F32)],
        compiler_params=_cparams(("parallel",), 40),
        name="dsa_prep",
    )(proj_f, proj_f, proj_f, proj_f, ta, ti, kn)


def _order_key(x):
    b = lax.bitcast_convert_type(x, I32)
    return jnp.where(b < 0, b ^ jnp.int32(0x7FFFFFFF), b)


def _key_value(k):
    return lax.bitcast_convert_type(jnp.where(k < 0, k ^ jnp.int32(0x7FFFFFFF), k), F32)


KEY_NEG_INF = -2139095041
KEY_POS_INF = 2139095040
SEARCH_COLLECT_MAX = 16
SEARCH_INTERP_ITERS = 24
SOFTMAX_MIN_SUM = 1e-25


def _lane_chunk_sum(m, tk):
    acc = m[:, 0:V7X_LANES]
    for c in range(1, tk // V7X_LANES):
        acc = acc + m[:, c * V7X_LANES:(c + 1) * V7X_LANES]
    return acc


def _dsa_kernel(qi_ref, wi_ref, q_ref, ki_ref, k_ref, v_ref, o_ref,
                key_ref, acc_ref, s_ref, m_ref, l_ref, kmax_ref, *, tq, tk, topk):
    i = pl.program_id(0)
    n_kt = (i * tq + tq + tk - 1) // tk
    qpos = i * tq + lax.broadcasted_iota(I32, (tq, 1), 0)
    wi = wi_ref[...]
    rep = N_ATT_HEADS // N_KV_HEADS

    @pl.when(i == 0)
    def _():
        for g in range(N_KV_HEADS):
            gs = slice(g * ATT_HEAD_DIM, (g + 1) * ATT_HEAD_DIM)

            def body(t, m):
                r0 = pl.multiple_of(t * tk, tk)
                kk = k_ref[pl.ds(r0, tk), gs].astype(F32)
                n2 = jnp.sum(kk * kk, axis=-1, keepdims=True)
                return jnp.maximum(m, jnp.max(n2, axis=0, keepdims=True))

            m = lax.fori_loop(0, k_ref.shape[0] // tk, body, jnp.zeros((1, 1), F32))
            kmax_ref[g] = jnp.broadcast_to(jnp.sqrt(m), (1, V7X_LANES))

    def score_tile(kt, carry):
        m1, m2 = carry
        k0 = pl.multiple_of(kt * tk, tk)
        kit = ki_ref[pl.ds(k0, tk), :]
        acc = jnp.zeros((tq, tk), F32)
        for h in range(N_IDX_HEADS):
            d = _dot_nt(qi_ref[h], kit)
            acc = acc + wi[:, h:h + 1] * jnp.maximum(d, 0.0)
        kpos = k0 + lax.broadcasted_iota(I32, (1, tk), 1)
        key = jnp.where(kpos <= qpos, _order_key(acc), jnp.int32(INT_MIN))
        key_ref[kt] = key
        for c in range(tk // V7X_LANES):
            x = key[:, c * V7X_LANES:(c + 1) * V7X_LANES]
            m2 = jnp.maximum(m2, jnp.minimum(m1, x))
            m1 = jnp.maximum(m1, x)
        return m1, m2

    lowest = jnp.full((tq, V7X_LANES), INT_MIN, I32)
    m1, m2 = lax.fori_loop(0, n_kt, score_tile, (lowest, lowest))
    row_max = jnp.max(m1, axis=-1, keepdims=True)

    def count_ge(thr):
        def body(kt, acc):
            m = jnp.where(key_ref[kt] >= thr, 1, 0).astype(I32)
            return acc + _lane_chunk_sum(m, tk)
        acc = lax.fori_loop(0, n_kt, body, jnp.zeros((tq, V7X_LANES), I32))
        return jnp.sum(acc, axis=-1, keepdims=True)

    log_target = float(np.log(topk - 0.5))

    def any_row(mask):
        return jnp.max(jnp.where(mask, 1, 0)) > 0

    def wide_bracket(st):
        lo, hi, cnt_lo, cnt_hi = st[1:5]
        return any_row((lo + 1 < hi) & (cnt_lo - cnt_hi > SEARCH_COLLECT_MAX))

    def open_bracket(st):
        return any_row(st[1] + 1 < st[2])

    def search_body(st):
        it, lo, hi, cnt_lo, cnt_hi, f_lo, f_hi, side = st
        active = lo + 1 < hi
        vlo = _key_value(lo)
        vhi = _key_value(hi)
        frac = f_lo / (f_lo - f_hi)
        interp = _order_key(vlo + frac * (vhi - vlo))
        bisect = (lo & hi) + ((lo ^ hi) >> 1)
        use_interp = ((lo > KEY_NEG_INF) & (hi <= KEY_POS_INF)
                      & ((it < SEARCH_INTERP_ITERS) | ((it & 1) == 0)))
        mid = jnp.where(use_interp, interp, bisect)
        mid = jnp.minimum(jnp.maximum(mid, lo + 1), hi - 1)
        mid = jnp.where(active, mid, lo)
        cnt = count_ge(mid)
        up = active & (cnt >= topk)
        down = active & (cnt < topk)
        hi = jnp.where(up & (cnt == topk), mid + 1, jnp.where(down, mid, hi))
        lo = jnp.where(up, mid, lo)
        cnt_lo = jnp.where(up, cnt, cnt_lo)
        cnt_hi = jnp.where(down, cnt, cnt_hi)
        f_new = jnp.log(jnp.maximum(cnt.astype(F32), 0.5)) - log_target
        f_lo = jnp.where(up, f_new, jnp.where(down & (side < 0), 0.5 * f_lo, f_lo))
        f_hi = jnp.where(down, f_new, jnp.where(up & (side > 0), 0.5 * f_hi, f_hi))
        side = jnp.where(up, 1, jnp.where(down, -1, side))
        return it + 1, lo, hi, cnt_lo, cnt_hi, f_lo, f_hi, side

    lo0 = jnp.min(m2, axis=-1, keepdims=True)
    cnt0 = count_ge(lo0)
    hi0 = jnp.where(cnt0 == topk, lo0 + 1, row_max + 1)
    zero = jnp.zeros((tq, 1), I32)
    f_lo0 = jnp.log(cnt0.astype(F32)) - log_target
    f_hi0 = jnp.full((tq, 1), float(np.log(0.5)) - log_target, F32)
    st = lax.while_loop(wide_bracket, search_body, (zero, lo0, hi0, cnt0, zero, f_lo0, f_hi0, zero))
    it, lo, hi, cnt_lo, cnt_hi = st[0:5]

    def collect_tile(kt, carry):
        c1, c2, c3 = carry
        x = key_ref[kt]
        x = jnp.where(x >= lo, jnp.where(x < hi, x, jnp.int32(INT_MIN)), jnp.int32(INT_MIN))
        for c in range(tk // V7X_LANES):
            t = x[:, c * V7X_LANES:(c + 1) * V7X_LANES]
            c1, t = jnp.maximum(c1, t), jnp.minimum(c1, t)
            c2, t = jnp.maximum(c2, t), jnp.minimum(c2, t)
            c3 = jnp.maximum(c3, t)
        return c1, c2, c3

    cand = jnp.concatenate(lax.fori_loop(0, n_kt, collect_tile, (lowest, lowest, lowest)), axis=1)
    got = jnp.sum(jnp.where(cand > INT_MIN, 1, 0), axis=-1, keepdims=True)
    complete = got == cnt_lo - cnt_hi

    def small_cond(s):
        return any_row(complete & (s[0] + 1 < s[1]))

    def small_body(s):
        lo, hi, cnt_lo = s
        active = complete & (lo + 1 < hi)
        mid = jnp.where(active, (lo & hi) + ((lo ^ hi) >> 1), lo)
        cnt = cnt_hi + jnp.sum(jnp.where(cand >= mid, 1, 0), axis=-1, keepdims=True)
        up = active & (cnt >= topk)
        down = active & (cnt < topk)
        hi = jnp.where(up & (cnt == topk), mid + 1, jnp.where(down, mid, hi))
        return jnp.where(up, mid, lo), hi, jnp.where(up, cnt, cnt_lo)

    lo, hi, cnt_lo = lax.while_loop(small_cond, small_body, (lo, hi, cnt_lo))
    st = lax.while_loop(open_bracket, search_body, (it, lo, hi, cnt_lo) + tuple(st[4:]))
    thr, cnt_thr = st[1], st[3]

    tie_rows = (cnt_thr > topk) & (thr > INT_MIN)

    @pl.when(jnp.max(jnp.where(tie_rows, 1, 0)) > 0)
    def _():
        need = topk - count_ge(thr + 1)
        a = lax.broadcasted_iota(I32, (tk, tk), 0)
        b = lax.broadcasted_iota(I32, (tk, tk), 1)
        before = jnp.where(a < b, 1.0, 0.0).astype(BF16)

        def body(kt, seen):
            key = key_ref[kt]
            eq = (key == thr) & tie_rows
            eqf = jnp.where(eq, 1.0, 0.0)
            rank = seen + _dot(eqf.astype(BF16), before)
            drop = eq & (rank >= need.astype(F32))
            key_ref[kt] = jnp.where(drop, jnp.int32(INT_MIN), key)
            return seen + jnp.sum(eqf, axis=-1, keepdims=True)

        lax.fori_loop(0, n_kt, body, jnp.zeros((tq, 1), F32))

    thr = jnp.maximum(thr, INT_MIN + 1)

    acc_ref[...] = jnp.zeros(acc_ref.shape, F32)
    for h in range(N_ATT_HEADS):
        qh = q_ref[:, h * ATT_HEAD_DIM:(h + 1) * ATT_HEAD_DIM].astype(F32)
        qn = jnp.sqrt(jnp.sum(qh * qh, axis=-1, keepdims=True))
        m_ref[h] = qn * kmax_ref[h // rep][:, 0:1]
    ones_col = jnp.where(lax.broadcasted_iota(I32, (tk, V7X_LANES), 1) == 0, 1.0, 0.0).astype(BF16)

    def attend_tile(kt, carry):
        k0 = pl.multiple_of(kt * tk, tk)
        keep = jnp.where(key_ref[kt] >= thr, 1.0, 0.0)
        for h in range(N_ATT_HEADS):
            gs = slice((h // rep) * ATT_HEAD_DIM, (h // rep + 1) * ATT_HEAD_DIM)
            qh = q_ref[:, h * ATT_HEAD_DIM:(h + 1) * ATT_HEAD_DIM]
            s_ref[h] = _dot_nt(qh, k_ref[pl.ds(k0, tk), gs])
        for h in range(N_ATT_HEADS):
            gs = slice((h // rep) * ATT_HEAD_DIM, (h // rep + 1) * ATT_HEAD_DIM)
            p = jnp.exp2(s_ref[h] - m_ref[h]) * keep
            v_aug = jnp.concatenate([v_ref[pl.ds(k0, tk), gs], ones_col], axis=1)
            acc_ref[h] += _dot(p.astype(BF16), v_aug)
        return carry

    lax.fori_loop(0, n_kt, attend_tile, 0)
    underflow = jnp.int32(0)
    for h in range(N_ATT_HEADS):
        l = acc_ref[h, :, ATT_HEAD_DIM:ATT_HEAD_DIM + 1]
        underflow = jnp.maximum(underflow, jnp.max(jnp.where(l >= SOFTMAX_MIN_SUM, 0, 1)))
        o_ref[:, h * ATT_HEAD_DIM:(h + 1) * ATT_HEAD_DIM] = (acc_ref[h, :, 0:ATT_HEAD_DIM] / l).astype(o_ref.dtype)

    @pl.when(underflow > 0)
    def _():
        _attend_online(q_ref, k_ref, v_ref, o_ref, key_ref, acc_ref, m_ref, l_ref, thr, n_kt, tk)


def _attend_online(q_ref, k_ref, v_ref, o_ref, key_ref, acc_ref, m_ref, l_ref, thr, n_kt, tk):
    rep = N_ATT_HEADS // N_KV_HEADS
    d = ATT_HEAD_DIM
    m_ref[...] = jnp.full(m_ref.shape, MASKED_LOGIT, F32)
    l_ref[...] = jnp.zeros(l_ref.shape, F32)
    acc_ref[...] = jnp.zeros(acc_ref.shape, F32)

    def online_tile(kt, carry):
        k0 = pl.multiple_of(kt * tk, tk)
        sel = key_ref[kt] >= thr
        for g in range(N_KV_HEADS):
            kg = k_ref[pl.ds(k0, tk), g * d:(g + 1) * d]
            vg = v_ref[pl.ds(k0, tk), g * d:(g + 1) * d]
            for r in range(rep):
                h = g * rep + r
                s = jnp.where(sel, _dot_nt(q_ref[:, h * d:(h + 1) * d], kg), MASKED_LOGIT)
                m_old = m_ref[h]
                m_new = jnp.maximum(m_old, jnp.max(s, axis=-1, keepdims=True))
                alpha = jnp.exp2(m_old - m_new)
                p = jnp.exp2(s - m_new)
                l_ref[h] = alpha * l_ref[h] + jnp.sum(p, axis=-1, keepdims=True)
                acc_ref[h, :, 0:d] = alpha * acc_ref[h, :, 0:d] + _dot(p.astype(BF16), vg)
                m_ref[h] = m_new
        return carry

    lax.fori_loop(0, n_kt, online_tile, 0)
    for h in range(N_ATT_HEADS):
        o_ref[:, h * d:(h + 1) * d] = (acc_ref[h, :, 0:d] / l_ref[h]).astype(o_ref.dtype)


def _dsa(qi_hm, wi, q, ki, k, v_src, v_col, tq, tk, topk):
    s = q.shape[0]
    kw = N_KV_HEADS * ATT_HEAD_DIM
    qw = N_ATT_HEADS * ATT_HEAD_DIM
    whole = lambda shape, idx: pl.BlockSpec(shape, idx, pipeline_mode=pl.Buffered(1))
    kern = functools.partial(_dsa_kernel, tq=tq, tk=tk, topk=topk)
    return pl.pallas_call(
        kern,
        grid=(s // tq,),
        in_specs=[pl.BlockSpec((N_IDX_HEADS, tq, IDX_HEAD_DIM), lambda i: (0, i, 0)),
                  pl.BlockSpec((tq, N_IDX_HEADS), lambda i: (i, 0)),
                  pl.BlockSpec((tq, qw), lambda i: (i, 0)),
                  whole((s, IDX_HEAD_DIM), lambda i: (0, 0)),
                  whole((s, kw), lambda i: (0, 0)),
                  whole((s, kw), lambda i: (0, v_col // kw))],
        out_specs=pl.BlockSpec((tq, qw), lambda i: (i, 0)),
        out_shape=jax.ShapeDtypeStruct((s, qw), BF16),
        scratch_shapes=[pltpu.VMEM((s // tk, tq, tk), I32),
                        pltpu.VMEM((N_ATT_HEADS, tq, 2 * ATT_HEAD_DIM), F32),
                        pltpu.VMEM((N_ATT_HEADS, tq, tk), F32),
                        pltpu.VMEM((N_ATT_HEADS, tq, 1), F32),
                        pltpu.VMEM((N_ATT_HEADS, tq, 1), F32),
                        pltpu.VMEM((N_KV_HEADS, 1, V7X_LANES), F32)],
        compiler_params=_cparams(("arbitrary",), 56),
        name="dsa",
    )(qi_hm, wi, q, ki, k, v_src)


HGRN_BLOCK = 256
HGRN_LEVELS = (32, 64, 128, 256)


def _hgrn_constants():
    tb = HGRN_BLOCK
    t = np.arange(tb)[:, None]
    u = np.arange(tb)[None, :]
    lower = (u <= t).astype(np.float32)
    mats = [lower]
    c0 = HGRN_LEVELS[0]
    ref0 = c0 * (t // c0) + c0 // 2 - 1
    mats.append(lower - (u <= ref0))
    for c in HGRN_LEVELS[1:-1]:
        ref = c * (t // c) + c // 2 - 1
        mats.append(lower - (u <= ref))
    level = np.full((tb, tb), -1.0, np.float32)
    level[(t // c0 == u // c0) & (u <= t)] = 0.0
    for li, c in enumerate(HGRN_LEVELS[1:], start=1):
        sel = (t // c == u // c) & (t % c >= c // 2) & (u % c < c // 2)
        level[sel] = float(li)
    return np.concatenate(mats, axis=0), level


def _hgrn_kernel(q_ref, f_ref, i_ref, g_ref, lb_ref, gn_ref, mats_ref, lvl_ref, o_ref, st_ref):
    @pl.when(pl.program_id(0) == 0)
    def _():
        st_ref[...] = jnp.zeros(st_ref.shape, F32)

    for h in range(N_HGRN_HEADS):
        hs = slice(h * HGRN_DIM, (h + 1) * HGRN_DIM)
        _hgrn_head(q_ref.at[:, hs], f_ref.at[:, hs], i_ref.at[:, hs], g_ref.at[:, hs], lb_ref.at[h],
                   gn_ref, mats_ref, lvl_ref, o_ref.at[:, hs], st_ref.at[h])


def _hgrn_head(q_ref, f_ref, i_ref, g_ref, lb_ref, gn_ref, mats_ref, lvl_ref, o_ref, st_ref):
    tb = HGRN_BLOCK
    d = HGRN_DIM
    q = q_ref[...]
    qf = q * _sigmoid(q)
    lb = lb_ref[...]
    f = lb + (1.0 - lb) * _sigmoid(f_ref[...])
    logf = jnp.log(f)
    kf = 1.0 - f
    v = i_ref[...].astype(BF16)

    g1 = logf.astype(BF16)
    r1 = logf - g1.astype(F32)
    g2 = r1.astype(BF16)
    g3 = (r1 - g2.astype(F32)).astype(BF16)
    sums = _dot(mats_ref[...], jnp.concatenate([g1, g2, g3], axis=1))
    sums = sums[:, 0:d] + sums[:, d:2 * d] + sums[:, 2 * d:3 * d]
    n_mid = len(HGRN_LEVELS) - 1
    cum = sums[0:tb]
    rel = [sums[(l + 1) * tb:(l + 2) * tb] for l in range(n_mid)]
    rel.append(cum - cum[tb // 2 - 1:tb // 2, :])

    lvl = lvl_ref[...]
    amat = jnp.zeros((tb, tb), F32)
    for l, e in enumerate(rel):
        cap = EXP_CLAMP if l == 0 else 0.0
        qs = (qf * jnp.exp(jnp.minimum(e, cap))).astype(BF16)
        ks = (kf * jnp.exp(jnp.minimum(-e, cap))).astype(BF16)
        amat = jnp.where(lvl == float(l), _dot_nt(qs, ks), amat)

    st = st_ref[...]
    o = _dot(amat.astype(BF16), v) + _dot_nt((qf * jnp.exp(cum)).astype(BF16), st.astype(BF16))
    last = cum[tb - 1:tb, :]
    ks_state = (kf * jnp.exp(last - cum)).astype(BF16)
    st_ref[...] = st * jnp.exp(last) + _dot_tn(v, ks_state)

    ms = jnp.mean(o * o, axis=-1, keepdims=True)
    y = o * lax.rsqrt(ms + NORM_EPS) * gn_ref[...]
    gate = g_ref[...].astype(F32)
    o_ref[...] = (y * (gate * _sigmoid(gate))).astype(o_ref.dtype)


def _hgrn(proj_f, proj_b, lb, gn, cols_f, cols_b):
    s = proj_f.shape[0]
    tb = HGRN_BLOCK
    d = HGRN_DIM
    mats_np, lvl_np = _hgrn_constants()
    mats = jnp.asarray(mats_np, BF16)
    lvl = jnp.asarray(lvl_np, F32)
    w = N_HGRN_HEADS * d
    col = lambda off: (lambda j: (j, off // w))
    const = lambda shape: pl.BlockSpec(shape, lambda j: (0,) * len(shape))
    return pl.pallas_call(
        _hgrn_kernel,
        grid=(s // tb,),
        in_specs=[pl.BlockSpec((tb, w), col(cols_f["hq"])),
                  pl.BlockSpec((tb, w), col(cols_f["hf"])),
                  pl.BlockSpec((tb, w), col(cols_f["hi"])),
                  pl.BlockSpec((tb, w), col(cols_b["hg"])),
                  const((N_HGRN_HEADS, 1, d)),
                  const((1, d)),
                  const(mats_np.shape),
                  const(lvl_np.shape)],
        out_specs=pl.BlockSpec((tb, w), lambda j: (j, 0)),
        out_shape=jax.ShapeDtypeStruct((s, w), BF16),
        scratch_shapes=[pltpu.VMEM((N_HGRN_HEADS, d, d), F32)],
        compiler_params=_cparams(("arbitrary",), 32),
        name="hgrn",
    )(proj_f, proj_f, proj_f, proj_b, lb.reshape(N_HGRN_HEADS, 1, d), gn.reshape(1, d), mats, lvl)


def _merge_kernel(att_ref, rec_ref, ga_ref, gr_ref, wpa_ref, wph_ref, o_ref):
    a = _dot(att_ref[...], wpa_ref[...])
    r = _dot(rec_ref[...], wph_ref[...])
    merged = _sigmoid(ga_ref[...].astype(F32)) * a + _sigmoid(gr_ref[...].astype(F32)) * r
    o_ref[...] = merged.astype(o_ref.dtype)


def _merge(att, rec, proj_b, gates_col, wpa, wph, tm):
    s, aw = att.shape
    rw = rec.shape[1]
    d = wpa.shape[1]
    const = lambda shape: pl.BlockSpec(shape, lambda i: (0,) * len(shape), pipeline_mode=pl.Buffered(1))
    row = lambda w: pl.BlockSpec((tm, w), lambda i: (i, 0))
    gcol = gates_col // d
    return pl.pallas_call(
        _merge_kernel,
        grid=(s // tm,),
        in_specs=[row(aw), row(rw),
                  pl.BlockSpec((tm, d), lambda i: (i, gcol)),
                  pl.BlockSpec((tm, d), lambda i: (i, gcol + 1)),
                  const((aw, d)), const((rw, d))],
        out_specs=row(d),
        out_shape=jax.ShapeDtypeStruct((s, d), BF16),
        compiler_params=_cparams(("parallel",), 48),
        name="merge",
    )(att, rec, proj_b, proj_b, wpa, wph)


def _post_kernel(mg_ref, x_ref, wo_ref, gt_ref, g2_ref, sc_ref, sh_ref, wr_ref, br_ref,
                 x1_ref, h2_ref, te_ref, tg_ref, sel_ref):
    mix = _dot(mg_ref[...], wo_ref[...])
    x1 = x_ref[...] + gt_ref[...] * mix
    x1_ref[...] = x1
    ms = jnp.mean(x1 * x1, axis=-1, keepdims=True)
    h2 = x1 * lax.rsqrt(ms + NORM_EPS) * g2_ref[...]
    h2 = h2 * (1.0 + sc_ref[...]) + sh_ref[...]
    h2_ref[...] = _pack_rows(h2)
    ne = br_ref.shape[-1]
    h_hi = h2.astype(BF16)
    h_lo = (h2 - h_hi.astype(F32)).astype(BF16)
    t = _dot(h_hi, wr_ref[...])
    logits = t[:, 0:ne] + t[:, ne:2 * ne] + _dot(h_lo, wr_ref[:, 0:ne]) + br_ref[...]
    lane = lax.broadcasted_iota(I32, logits.shape, 1)
    work = logits
    vals, idxs = [], []
    sel = jnp.zeros(logits.shape, F32)
    for _ in range(MOE_TOPK):
        m = jnp.max(work, axis=-1, keepdims=True)
        idx = jnp.min(jnp.where(work == m, lane, ne), axis=-1, keepdims=True)
        hit = lane == idx
        vals.append(m)
        idxs.append(idx)
        sel = jnp.where(hit, 1.0, sel)
        work = jnp.where(hit, -jnp.inf, work)
    es = [jnp.exp(vk - vals[0]) for vk in vals]
    denom = es[0]
    for e in es[1:]:
        denom = denom + e
    lane4 = lax.broadcasted_iota(I32, te_ref.shape, 1)
    te = jnp.zeros(te_ref.shape, I32)
    tg = jnp.zeros(tg_ref.shape, F32)
    for kk in range(MOE_TOPK):
        te = jnp.where(lane4 == kk, idxs[kk], te)
        tg = jnp.where(lane4 == kk, es[kk] / denom, tg)
    te_ref[...] = te
    tg_ref[...] = tg
    sel_ref[...] = sel.astype(sel_ref.dtype)


def _post_mixer(merged, x, wo, gt1, g2, sc2, sh2, wr, br, tm):
    s, d = x.shape
    ne = br.shape[1]
    wr_hi = wr.astype(BF16)
    wr = jnp.concatenate([wr_hi, (wr - wr_hi.astype(F32)).astype(BF16)], axis=1)
    const = lambda shape: pl.BlockSpec(shape, lambda i: (0,) * len(shape), pipeline_mode=pl.Buffered(1))
    row = lambda w: pl.BlockSpec((tm, w), lambda i: (i, 0))
    return pl.pallas_call(
        _post_kernel,
        grid=(s // tm,),
        in_specs=[row(d), row(d), const((d, d)),
                  const((1, d)), const((1, d)), const((1, d)), const((1, d)),
                  const((d, 2 * ne)), const((1, ne))],
        out_specs=[row(d), row(d // 2), row(MOE_TOPK), row(MOE_TOPK), row(ne)],
        out_shape=[jax.ShapeDtypeStruct((s, d), F32),
                   jax.ShapeDtypeStruct((s, d // 2), I32),
                   jax.ShapeDtypeStruct((s, MOE_TOPK), I32),
                   jax.ShapeDtypeStruct((s, MOE_TOPK), F32),
                   jax.ShapeDtypeStruct((s, ne), BF16)],
        compiler_params=_cparams(("parallel",), 56),
        name="post_mixer",
    )(merged, x, wo, gt1, g2, sc2, sh2, wr, br)


def _rank_kernel(sel_ref, rank_ref, cnt_ref, carry_ref):
    @pl.when(pl.program_id(0) == 0)
    def _():
        carry_ref[...] = jnp.zeros(carry_ref.shape, F32)

    sel = sel_ref[...]
    tt = sel.shape[0]
    a = lax.broadcasted_iota(I32, (tt, tt), 0)
    b = lax.broadcasted_iota(I32, (tt, tt), 1)
    before = jnp.where(b < a, 1.0, 0.0).astype(BF16)
    carry = carry_ref[...]
    rank_ref[...] = _dot(before, sel) + carry
    carry = carry + jnp.sum(sel.astype(F32), axis=0, keepdims=True)
    carry_ref[...] = carry
    cnt_ref[...] = carry


def _expert_ranks(sel, tt):
    s, ne = sel.shape
    return pl.pallas_call(
        _rank_kernel,
        grid=(s // tt,),
        in_specs=[pl.BlockSpec((tt, ne), lambda i: (i, 0))],
        out_specs=[pl.BlockSpec((tt, ne), lambda i: (i, 0)),
                   pl.BlockSpec((1, ne), lambda i: (0, 0))],
        out_shape=[jax.ShapeDtypeStruct((s, ne), F32),
                   jax.ShapeDtypeStruct((1, ne), F32)],
        scratch_shapes=[pltpu.VMEM((1, ne), F32)],
        compiler_params=_cparams(("arbitrary",), 16),
        name="expert_ranks",
    )(sel)


def _row_copy(src_ref, src_row, dst_ref, dst_row, sem):
    return pltpu.make_async_copy(src_ref.at[pl.ds(src_row, 1)], dst_ref.at[pl.ds(dst_row, 1)], sem)


def _scatter_kernel(dest_ref, h_ref, xb_in_ref, xb_ref, sem, *, tt):
    del xb_in_ref
    base = pl.program_id(0) * tt * MOE_TOPK

    def issue(t, c):
        for kk in range(MOE_TOPK):
            _row_copy(h_ref, t, xb_ref, dest_ref[base + t * MOE_TOPK + kk], sem).start()
        return c

    lax.fori_loop(0, tt, issue, 0)

    def drain(t, c):
        for kk in range(MOE_TOPK):
            _row_copy(h_ref, 0, xb_ref, 0, sem).wait()
        return c

    lax.fori_loop(0, tt, drain, 0)


def _scatter_rows(dest_flat, h2, n_rows, tt):
    s, d = h2.shape
    xb0 = jnp.zeros((n_rows, d), h2.dtype)
    grid_spec = pltpu.PrefetchScalarGridSpec(
        num_scalar_prefetch=1,
        grid=(s // tt,),
        in_specs=[pl.BlockSpec((tt, d), lambda i, dest: (i, 0)),
                  pl.BlockSpec(memory_space=pl.ANY)],
        out_specs=pl.BlockSpec(memory_space=pl.ANY),
        scratch_shapes=[pltpu.SemaphoreType.DMA(())],
    )
    return pl.pallas_call(
        functools.partial(_scatter_kernel, tt=tt),
        grid_spec=grid_spec,
        out_shape=jax.ShapeDtypeStruct((n_rows, d), h2.dtype),
        input_output_aliases={2: 0},
        compiler_params=_cparams(("arbitrary",), 32),
        name="moe_scatter",
    )(dest_flat, h2, xb0)


ITEM_UNUSED, ITEM_USED, ITEM_FIRST = 0, 1, 3


def _unpack_rows(words):
    lo = lax.bitcast_convert_type(words << 16, F32)
    hi = lax.bitcast_convert_type(words & jnp.int32(-65536), F32)
    return jnp.concatenate([lo.astype(BF16), hi.astype(BF16)], axis=1)


def _pack_rows(x):
    half = x.shape[1] // 2
    lo = lax.bitcast_convert_type(x[:, :half].astype(BF16).astype(F32), I32)
    hi = lax.bitcast_convert_type(x[:, half:].astype(BF16).astype(F32), I32)
    return lax.shift_right_logical(lo, jnp.int32(16)) | (hi & jnp.int32(-65536))


def _expert_up_kernel(sb_ref, wc_ref, oc_ref, e_ref, flag_ref, val_ref,
                      x_ref, w1g_ref, w1l_ref, b1g_ref, b1l_ref, a_ref, wg_ref, wl_ref):
    i = pl.program_id(0)
    flag = flag_ref[i]

    @pl.when(flag == ITEM_FIRST)
    def _():
        wg_ref[...] = w1g_ref[0].astype(BF16)
        wl_ref[...] = w1l_ref[0].astype(BF16)

    @pl.when(flag != ITEM_UNUSED)
    def _():
        rows = lax.broadcasted_iota(I32, (x_ref.shape[0], 1), 0)
        x = _unpack_rows(jnp.where(rows < val_ref[i], x_ref[...], 0))
        ug = _dot(x, wg_ref[...]) + b1g_ref[0]
        ul = _dot(x, wl_ref[...]) + b1l_ref[0]
        ug = jnp.minimum(ug, SWIGLU_LIMIT)
        ul = jnp.clip(ul, -SWIGLU_LIMIT, SWIGLU_LIMIT)
        a_ref[...] = (ug * _sigmoid(SWIGLU_ALPHA * ug) * (ul + 1.0)).astype(a_ref.dtype)

    @pl.when(flag == ITEM_UNUSED)
    def _():
        a_ref[...] = jnp.zeros(a_ref.shape, a_ref.dtype)


def _expert_down_kernel(sb_ref, wc_ref, oc_ref, e_ref, flag_ref, val_ref,
                        a_ref, w2_ref, b2_ref, y_ref, w_ref):
    i = pl.program_id(0)
    flag = flag_ref[i]

    @pl.when(flag == ITEM_FIRST)
    def _():
        w_ref[...] = w2_ref[0].astype(BF16)

    @pl.when(flag != ITEM_UNUSED)
    def _():
        y_ref[...] = _dot(a_ref[...], w_ref[...]) + b2_ref[0]

    @pl.when(flag == ITEM_UNUSED)
    def _():
        y_ref[...] = jnp.zeros(y_ref.shape, y_ref.dtype)


def _expert_items(sb_expert, sb_valid, n_used, pstart, pcnt, n_sb, rb, nch):
    idx = jnp.arange(n_sb * nch, dtype=I32)
    nu = n_used[0]
    used = idx < nu * nch
    e = sb_expert[jnp.minimum(idx // nch, nu - 1)]
    first_sb = pstart[e] // rb
    n_blk = jnp.maximum(pcnt[e] // rb, 1)
    local = idx - nch * first_sb
    n_free = jnp.maximum(n_sb - nu, 1)
    spare = idx - nch * nu
    item_sb = jnp.where(used, first_sb + local % n_blk, nu + spare % n_free)
    chunk = jnp.where(used, local // n_blk, spare // n_free)
    item_wc = jnp.where(used, chunk, nch - 1)
    flag = jnp.where(used, jnp.where(local % n_blk == 0, ITEM_FIRST, ITEM_USED), ITEM_UNUSED)
    return (item_sb.astype(I32), item_wc.astype(I32), chunk.astype(I32), e.astype(I32),
            flag.astype(I32), sb_valid[item_sb].astype(I32))


def _experts(items, xb, w1, b1, w2, b2, rb, th):
    n_rows = xb.shape[0]
    ne, d, two_de = w1.shape
    de = two_de // 2
    nch = de // th
    n_items = items[0].shape[0]
    b1r = b1.reshape(ne, 1, two_de)
    up_spec = pltpu.PrefetchScalarGridSpec(
        num_scalar_prefetch=6,
        grid=(n_items,),
        in_specs=[pl.BlockSpec((rb, d // 2), lambda i, sb, wc, oc, e, f, v: (sb[i], 0)),
                  pl.BlockSpec((1, d, th), lambda i, sb, wc, oc, e, f, v: (e[i], 0, wc[i])),
                  pl.BlockSpec((1, d, th), lambda i, sb, wc, oc, e, f, v: (e[i], 0, nch + wc[i])),
                  pl.BlockSpec((1, 1, th), lambda i, sb, wc, oc, e, f, v: (e[i], 0, wc[i])),
                  pl.BlockSpec((1, 1, th), lambda i, sb, wc, oc, e, f, v: (e[i], 0, nch + wc[i]))],
        out_specs=pl.BlockSpec((rb, th), lambda i, sb, wc, oc, e, f, v: (sb[i], oc[i])),
        scratch_shapes=[pltpu.VMEM((d, th), BF16), pltpu.VMEM((d, th), BF16)],
    )
    act = pl.pallas_call(
        _expert_up_kernel,
        grid_spec=up_spec,
        out_shape=jax.ShapeDtypeStruct((n_rows, de), BF16),
        compiler_params=_cparams(("arbitrary",), 58),
        name="moe_expert_up",
    )(*items, xb, w1, w1, b1r, b1r)
    tn = d // nch
    down_spec = pltpu.PrefetchScalarGridSpec(
        num_scalar_prefetch=6,
        grid=(n_items,),
        in_specs=[pl.BlockSpec((rb, de), lambda i, sb, wc, oc, e, f, v: (sb[i], 0)),
                  pl.BlockSpec((1, de, tn), lambda i, sb, wc, oc, e, f, v: (e[i], 0, wc[i])),
                  pl.BlockSpec((1, 1, tn), lambda i, sb, wc, oc, e, f, v: (e[i], 0, wc[i]))],
        out_specs=pl.BlockSpec((rb, tn), lambda i, sb, wc, oc, e, f, v: (sb[i], oc[i])),
        scratch_shapes=[pltpu.VMEM((de, tn), BF16)],
    )
    return pl.pallas_call(
        _expert_down_kernel,
        grid_spec=down_spec,
        out_shape=jax.ShapeDtypeStruct((n_rows, d), F32),
        compiler_params=_cparams(("arbitrary",), 48),
        name="moe_expert_down",
    )(*items, act, w2, b2.reshape(ne, 1, d))


def _combine_kernel(dest_ref, yb_ref, x1_ref, tg_ref, gt_ref, gn_ref, o_ref, buf_ref, sem, *, tt):
    i = pl.program_id(0)
    slot = i % 2

    def gather(step, into):
        base = step * tt * MOE_TOPK

        def issue(t, c):
            for kk in range(MOE_TOPK):
                _row_copy(yb_ref, dest_ref[base + t * MOE_TOPK + kk], buf_ref.at[into, kk], t,
                          sem.at[into]).start()
            return c

        lax.fori_loop(0, tt, issue, 0)

    @pl.when(i == 0)
    def _():
        gather(0, 0)

    @pl.when(i + 1 < pl.num_programs(0))
    def _():
        gather(i + 1, 1 - slot)

    def drain(t, c):
        for kk in range(MOE_TOPK):
            _row_copy(yb_ref, 0, buf_ref.at[slot, kk], 0, sem.at[slot]).wait()
        return c

    lax.fori_loop(0, tt, drain, 0)

    tg = tg_ref[...]
    moe = tg[:, 0:1] * buf_ref[slot, 0]
    for kk in range(1, MOE_TOPK):
        moe = moe + tg[:, kk:kk + 1] * buf_ref[slot, kk]
    x2 = x1_ref[...] + gt_ref[...] * moe
    ms = jnp.mean(x2 * x2, axis=-1, keepdims=True)
    o_ref[...] = x2 * lax.rsqrt(ms + NORM_EPS) * gn_ref[...]


def _combine(dest_flat, yb, x1, tg, gt2, gn, tt):
    s, d = x1.shape
    grid_spec = pltpu.PrefetchScalarGridSpec(
        num_scalar_prefetch=1,
        grid=(s // tt,),
        in_specs=[pl.BlockSpec(memory_space=pl.ANY),
                  pl.BlockSpec((tt, d), lambda i, dest: (i, 0)),
                  pl.BlockSpec((tt, MOE_TOPK), lambda i, dest: (i, 0)),
                  pl.BlockSpec((1, d), lambda i, dest: (0, 0)),
                  pl.BlockSpec((1, d), lambda i, dest: (0, 0))],
        out_specs=pl.BlockSpec((tt, d), lambda i, dest: (i, 0)),
        scratch_shapes=[pltpu.VMEM((2, MOE_TOPK, tt, d), F32),
                        pltpu.SemaphoreType.DMA((2,))],
    )
    return pl.pallas_call(
        functools.partial(_combine_kernel, tt=tt),
        grid_spec=grid_spec,
        out_shape=jax.ShapeDtypeStruct((s, d), F32),
        compiler_params=_cparams(("arbitrary",), 40),
        name="moe_combine",
    )(dest_flat, yb, x1, tg, gt2, gn)


MOE_ROW_BLOCK = 512
MOE_HIDDEN_TILE = 1024


def _pad_cols(w, n):
    return jnp.pad(w, ((0, 0), (0, n - w.shape[1])))


def kernel(x, c, positions, w_ada, b_ada, norm1_g, w_in, idx_knorm_g, idx_knorm_b, hgrn_lb_logits,
           hgrn_norm_g, w_proj_att, w_proj_hgrn, w_out, norm2_g, w_router, b_router, w1, b1, w2, b2,
           final_norm_g):
    bsz, s, d = x.shape
    assert bsz == 1 and w_ada.shape[0] == 1
    xs = x.reshape(s, d)
    pos = positions.reshape(s)

    mod = _adaln(c, w_ada[0], b_ada[0])
    sh1, sc1, gt1, sh2, sc2, gt2 = [mod[:, i * d:(i + 1) * d] for i in range(6)]

    widths = dict(q=N_ATT_HEADS * ATT_HEAD_DIM, k=N_KV_HEADS * ATT_HEAD_DIM, v=N_KV_HEADS * ATT_HEAD_DIM,
                  qi=N_IDX_HEADS * IDX_HEAD_DIM, ki=IDX_HEAD_DIM, wi=N_IDX_HEADS,
                  hq=N_HGRN_HEADS * HGRN_DIM, hf=N_HGRN_HEADS * HGRN_DIM, hi=N_HGRN_HEADS * HGRN_DIM,
                  hg=N_HGRN_HEADS * HGRN_DIM, gates=2 * d)
    src, off = {}, 0
    for name in ("q", "k", "v", "qi", "ki", "wi", "hq", "hf", "hi", "hg", "gates"):
        src[name] = (off, widths[name])
        off += widths[name]
    w0 = w_in[0]
    take = lambda name: lax.slice_in_dim(w0, src[name][0], src[name][0] + src[name][1], axis=1)
    kiwi = _pad_cols(jnp.concatenate([take("ki"), take("wi")], axis=1), V7X_LANES)
    f_names = ("q", "qi", "hq", "hf", "hi", "k")
    f_parts = [take(n) for n in f_names] + [kiwi]
    cols_f, o = {}, 0
    for n, p in zip(f_names + ("kiwi",), f_parts):
        cols_f[n] = o
        o += p.shape[1]
    tn_f = 2816
    w_f = _pad_cols(jnp.concatenate(f_parts, axis=1), -(-o // tn_f) * tn_f).astype(BF16)
    b_names = ("gates", "hg", "v")
    cols_b, o = {}, 0
    for n in b_names:
        cols_b[n] = o
        o += widths[n]
    w_b = jnp.concatenate([take(n) for n in b_names], axis=1).astype(BF16)

    g1 = norm1_g[0].reshape(1, d)
    proj_f = _norm_matmul(xs, g1, sc1, sh1, w_f, F32, 512, tn_f, "in_proj_f32")
    proj_b = _norm_matmul(xs, g1, sc1, sh1, w_b, BF16, 512, 1792, "in_proj_bf16")

    q_r, k_r, qi_hm, ki_r, wi_r = _dsa_prep(proj_f, pos, idx_knorm_g[0], idx_knorm_b[0], cols_f, 512)
    topk = min(IDX_TOPK, s // 4)
    att = _dsa(qi_hm, wi_r, q_r, ki_r, k_r, proj_b, cols_b["v"], 128, 1024, topk)

    lb_all = jnp.cumsum(jax.nn.softmax(hgrn_lb_logits.astype(F32), axis=0), axis=0)
    rec = _hgrn(proj_f, proj_b, lb_all[0], hgrn_norm_g[0], cols_f, cols_b)

    merged = _merge(att, rec, proj_b, cols_b["gates"],
                    w_proj_att[0].astype(BF16), w_proj_hgrn[0].astype(BF16), 512)
    x1, h2, top_e, top_g, sel = _post_mixer(
        merged, xs, w_out[0].astype(BF16),
        gt1, norm2_g[0].reshape(1, d), sc2, sh2, w_router[0], b_router[0].reshape(1, -1), 512)

    rank, counts = _expert_ranks(sel, 512)
    rb = MOE_ROW_BLOCK
    n_sb = s * MOE_TOPK // rb + N_EXPERTS
    cnt = counts[0].astype(I32)
    pcnt = (cnt + rb - 1) // rb * rb
    pend = jnp.cumsum(pcnt)
    pstart = pend - pcnt
    dest = jnp.take_along_axis(rank.astype(I32) + pstart[None, :], top_e, axis=1).reshape(-1)
    sb_row = jnp.arange(n_sb, dtype=I32) * rb
    sb_expert = jnp.minimum(jnp.sum((pend[None, :] <= sb_row[:, None]).astype(I32), axis=1), N_EXPERTS - 1)
    sb_valid = jnp.clip(cnt[sb_expert] - (sb_row - pstart[sb_expert]), 0, rb).astype(I32)
    n_used = (pend[-1] // rb).astype(I32).reshape(1)

    xb = _scatter_rows(dest, h2, n_sb * rb, 512)
    items = _expert_items(sb_expert, sb_valid, n_used, pstart, pcnt, n_sb, rb, w2.shape[2] // MOE_HIDDEN_TILE)
    yb = _experts(items, xb, w1[0], b1[0], w2[0], b2[0], rb, MOE_HIDDEN_TILE)
    out = _combine(dest, yb, x1, top_g, gt2, final_norm_g.reshape(1, d), 256)
    return out.reshape(bsz, s, d)
```

```python
import functools

import numpy as np
import jax
import jax.numpy as jnp
from jax import lax
from jax.experimental import pallas as pl
from jax.experimental.pallas import tpu as pltpu

F32 = jnp.float32
BF16 = jnp.bfloat16
I32 = jnp.int32

ATT_HEAD_DIM = 128
N_ATT_HEADS = 8
N_KV_HEADS = 2
ATT_ROPE_DIM = 32
IDX_HEAD_DIM = 64
N_IDX_HEADS = 16
IDX_ROPE_DIM = 16
IDX_TOPK = 256
ROPE_THETA = 500000.0
HGRN_DIM = 128
N_HGRN_HEADS = 8
N_EXPERTS = 32
MOE_TOPK = 4
SWIGLU_LIMIT = 7.0
SWIGLU_ALPHA = 1.702
NORM_EPS = 1e-5

V7X_LANES = 128
INT_MIN = -2147483648
MASKED_LOGIT = -1e30
EXP_CLAMP = 80.0
LOG2_E = 1.4426950408889634

MIB = 1024 * 1024


def _cparams(sem, vmem_mib):
    return pltpu.CompilerParams(dimension_semantics=sem, vmem_limit_bytes=vmem_mib * MIB)


def _sigmoid(x):
    return 1.0 / (1.0 + jnp.exp(-x))


def _dot(a, b):
    return jnp.dot(a, b, preferred_element_type=F32)


def _dot_nt(a, b):
    return lax.dot_general(a, b, (((1,), (1,)), ((), ())), preferred_element_type=F32)


def _dot_tn(a, b):
    return lax.dot_general(a, b, (((0,), (0,)), ((), ())), preferred_element_type=F32)


def _adaln_kernel(c_ref, w_ref, b_ref, o_ref):
    c = c_ref[...]
    cs = c * _sigmoid(c)
    o_ref[...] = jnp.dot(cs, w_ref[...], preferred_element_type=F32,
                         precision=lax.Precision.HIGHEST) + b_ref[...]


def _adaln(c, w, b):
    d, n = w.shape
    tn = 1024
    c8 = jnp.broadcast_to(c, (8, d))
    out = pl.pallas_call(
        _adaln_kernel,
        grid=(n // tn,),
        in_specs=[pl.BlockSpec((8, d), lambda j: (0, 0)),
                  pl.BlockSpec((d, tn), lambda j: (0, j)),
                  pl.BlockSpec((1, tn), lambda j: (0, j))],
        out_specs=pl.BlockSpec((8, tn), lambda j: (0, j)),
        out_shape=jax.ShapeDtypeStruct((8, n), F32),
        compiler_params=_cparams(("arbitrary",), 40),
        name="adaln",
    )(c8, w, b.reshape(1, n))
    return out[0:1]


def _norm_matmul_kernel(x_ref, g_ref, sc_ref, sh_ref, w_ref, o_ref, h_ref):
    @pl.when(pl.program_id(1) == 0)
    def _():
        x = x_ref[...]
        ms = jnp.mean(x * x, axis=-1, keepdims=True)
        y = x * lax.rsqrt(ms + NORM_EPS) * g_ref[...]
        h_ref[...] = (y * (1.0 + sc_ref[...]) + sh_ref[...]).astype(BF16)

    o_ref[...] = _dot(h_ref[...], w_ref[...]).astype(o_ref.dtype)


def _norm_matmul(x, g, sc, sh, w, out_dtype, tm, tn, name):
    s, d = x.shape
    n = w.shape[1]
    vec = pl.BlockSpec((1, d), lambda i, j: (0, 0))
    return pl.pallas_call(
        _norm_matmul_kernel,
        grid=(s // tm, n // tn),
        in_specs=[pl.BlockSpec((tm, d), lambda i, j: (i, 0)), vec, vec, vec,
                  pl.BlockSpec((d, tn), lambda i, j: (0, j))],
        out_specs=[pl.BlockSpec((tm, tn), lambda i, j: (i, j)),
                   pl.BlockSpec((tm, d), lambda i, j: (i, 0))],
        out_shape=[jax.ShapeDtypeStruct((s, n), out_dtype),
                   jax.ShapeDtypeStruct((s, d), BF16)],
        compiler_params=_cparams(("parallel", "arbitrary"), 56),
        name=name,
    )(x, g, sc, sh, w)


def _matmul_kernel(h_ref, w_ref, o_ref):
    o_ref[...] = _dot(h_ref[...], w_ref[...]).astype(o_ref.dtype)


def _matmul(h, w, out_dtype, tm, tn, name):
    s, d = h.shape
    n = w.shape[1]
    return pl.pallas_call(
        _matmul_kernel,
        grid=(s // tm, n // tn),
        in_specs=[pl.BlockSpec((tm, d), lambda i, j: (i, 0)),
                  pl.BlockSpec((d, tn), lambda i, j: (0, j))],
        out_specs=pl.BlockSpec((tm, tn), lambda i, j: (i, j)),
        out_shape=jax.ShapeDtypeStruct((s, n), out_dtype),
        compiler_params=_cparams(("parallel", "arbitrary"), 48),
        name=name,
    )(h, w)


def _rope(x, a, bm, bp, half):
    n = x.shape[-1]
    return x * a + pltpu.roll(x, n - half, 1) * bm + pltpu.roll(x, half, 1) * bp


def _prep_kernel(q_ref, qi_ref, k_ref, kw_ref, ta_ref, ti_ref, kn_ref,
                 qo_ref, ko_ref, qio_ref, kio_ref, wio_ref):
    aa, abm, abp = ta_ref[0], ta_ref[1], ta_ref[2]
    ia, ibm, ibp = ti_ref[0], ti_ref[1], ti_ref[2]
    ah = ATT_ROPE_DIM // 2
    ih = IDX_ROPE_DIM // 2
    scale = ATT_HEAD_DIM ** -0.5 * LOG2_E
    for h in range(N_ATT_HEADS):
        sl = slice(h * ATT_HEAD_DIM, (h + 1) * ATT_HEAD_DIM)
        qo_ref[:, sl] = (_rope(q_ref[:, sl], aa, abm, abp, ah) * scale).astype(BF16)
    for h in range(N_KV_HEADS):
        sl = slice(h * ATT_HEAD_DIM, (h + 1) * ATT_HEAD_DIM)
        ko_ref[:, sl] = _rope(k_ref[:, sl], aa, abm, abp, ah).astype(BF16)
    for p in range(N_IDX_HEADS // 2):
        sl = slice(p * V7X_LANES, (p + 1) * V7X_LANES)
        y = _rope(qi_ref[:, sl], ia, ibm, ibp, ih).astype(BF16)
        qio_ref[2 * p] = y[:, :IDX_HEAD_DIM]
        qio_ref[2 * p + 1] = y[:, IDX_HEAD_DIM:]
    kw = kw_ref[...]
    lane = lax.broadcasted_iota(I32, kw.shape, 1)
    is_k = lane < IDX_HEAD_DIM
    mu = jnp.sum(jnp.where(is_k, kw, 0.0), axis=-1, keepdims=True) * (1.0 / IDX_HEAD_DIM)
    cen = kw - mu
    var = jnp.sum(jnp.where(is_k, cen * cen, 0.0), axis=-1, keepdims=True) * (1.0 / IDX_HEAD_DIM)
    kn = cen * lax.rsqrt(var + NORM_EPS) * kn_ref[0:1, :] + kn_ref[1:2, :]
    kn = _rope(kn, ia, ibm, ibp, ih)
    kio_ref[...] = kn[:, :IDX_HEAD_DIM].astype(BF16)
    wscale = N_IDX_HEADS ** -0.5 * IDX_HEAD_DIM ** -0.5
    wio_ref[...] = kw[:, IDX_HEAD_DIM:IDX_HEAD_DIM + N_IDX_HEADS] * wscale


def _rope_tables(pos, rot_dim, head_dim):
    half = rot_dim // 2
    inv_freq = jnp.power(ROPE_THETA, -jnp.arange(half, dtype=F32) * 2.0 / rot_dim)
    ang = pos.astype(F32)[:, None] * inv_freq
    cos, sin = jnp.cos(ang), jnp.sin(ang)
    s = pos.shape[0]
    rest = head_dim - rot_dim
    a = jnp.concatenate([cos, cos, jnp.ones((s, rest), F32)], axis=1)
    bm = jnp.concatenate([-sin, jnp.zeros((s, head_dim - half), F32)], axis=1)
    bp = jnp.concatenate([jnp.zeros((s, half), F32), sin, jnp.zeros((s, rest), F32)], axis=1)
    reps = V7X_LANES // head_dim
    return jnp.stack([jnp.tile(t, (1, reps)) for t in (a, bm, bp)])


def _dsa_prep(proj_f, pos, knorm_g, knorm_b, cols, tb):
    s = proj_f.shape[0]
    ta = _rope_tables(pos, ATT_ROPE_DIM, ATT_HEAD_DIM)
    ti = _rope_tables(pos, IDX_ROPE_DIM, IDX_HEAD_DIM)
    kn = jnp.zeros((2, V7X_LANES), F32)
    kn = kn.at[0, :IDX_HEAD_DIM].set(knorm_g).at[1, :IDX_HEAD_DIM].set(knorm_b)
    qw = N_ATT_HEADS * ATT_HEAD_DIM
    qiw = N_IDX_HEADS * IDX_HEAD_DIM
    kw = N_KV_HEADS * ATT_HEAD_DIM
    tab = pl.BlockSpec((3, tb, V7X_LANES), lambda i: (0, i, 0))
    return pl.pallas_call(
        _prep_kernel,
        grid=(s // tb,),
        in_specs=[pl.BlockSpec((tb, qw), lambda i: (i, cols["q"] // qw)),
                  pl.BlockSpec((tb, qiw), lambda i: (i, cols["qi"] // qiw)),
                  pl.BlockSpec((tb, kw), lambda i: (i, cols["k"] // kw)),
                  pl.BlockSpec((tb, V7X_LANES), lambda i: (i, cols["kiwi"] // V7X_LANES)),
                  tab, tab,
                  pl.BlockSpec((2, V7X_LANES), lambda i: (0, 0))],
        out_specs=[pl.BlockSpec((tb, qw), lambda i: (i, 0)),
                   pl.BlockSpec((tb, kw), lambda i: (i, 0)),
                   pl.BlockSpec((N_IDX_HEADS, tb, IDX_HEAD_DIM), lambda i: (0, i, 0)),
                   pl.BlockSpec((tb, IDX_HEAD_DIM), lambda i: (i, 0)),
                   pl.BlockSpec((tb, N_IDX_HEADS), lambda i: (i, 0))],
        out_shape=[jax.ShapeDtypeStruct((s, qw), BF16),
                   jax.ShapeDtypeStruct((s, kw), BF16),
                   jax.ShapeDtypeStruct((N_IDX_HEADS, s, IDX_HEAD_DIM), BF16),
                   jax.ShapeDtypeStruct((s, IDX_HEAD_DIM), BF16),
                   jax.ShapeDtypeStruct((s, N_IDX_HEADS), F32)],
        compiler_params=_cparams(("parallel",), 40),
        name="dsa_prep",
    )(proj_f, proj_f, proj_f, proj_f, ta, ti, kn)


def _order_key(x):
    b = lax.bitcast_convert_type(x, I32)
    return jnp.where(b < 0, b ^ jnp.int32(0x7FFFFFFF), b)


def _key_value(k):
    return lax.bitcast_convert_type(jnp.where(k < 0, k ^ jnp.int32(0x7FFFFFFF), k), F32)


KEY_NEG_INF = -2139095041
KEY_POS_INF = 2139095040
SEARCH_COLLECT_MAX = 16
SEARCH_INTERP_ITERS = 24
SOFTMAX_MIN_SUM = 1e-25


def _lane_chunk_sum(m, tk):
    acc = m[:, 0:V7X_LANES]
    for c in range(1, tk // V7X_LANES):
        acc = acc + m[:, c * V7X_LANES:(c + 1) * V7X_LANES]
    return acc


def _dsa_kernel(qi_ref, wi_ref, q_ref, ki_ref, k_ref, v_ref, o_ref,
                key_ref, acc_ref, s_ref, m_ref, l_ref, kmax_ref, *, tq, tk, topk):
    i = pl.program_id(0)
    n_kt = (i * tq + tq + tk - 1) // tk
    qpos = i * tq + lax.broadcasted_iota(I32, (tq, 1), 0)
    wi = wi_ref[...]
    rep = N_ATT_HEADS // N_KV_HEADS

    @pl.when(i == 0)
    def _():
        for g in range(N_KV_HEADS):
            gs = slice(g * ATT_HEAD_DIM, (g + 1) * ATT_HEAD_DIM)

            def body(t, m):
                r0 = pl.multiple_of(t * tk, tk)
                kk = k_ref[pl.ds(r0, tk), gs].astype(F32)
                n2 = jnp.sum(kk * kk, axis=-1, keepdims=True)
                return jnp.maximum(m, jnp.max(n2, axis=0, keepdims=True))

            m = lax.fori_loop(0, k_ref.shape[0] // tk, body, jnp.zeros((1, 1), F32))
            kmax_ref[g] = jnp.broadcast_to(jnp.sqrt(m), (1, V7X_LANES))

    def score_tile(kt, carry):
        m1, m2 = carry
        k0 = pl.multiple_of(kt * tk, tk)
        kit = ki_ref[pl.ds(k0, tk), :]
        acc = jnp.zeros((tq, tk), F32)
        for h in range(N_IDX_HEADS):
            d = _dot_nt(qi_ref[h], kit)
            acc = acc + wi[:, h:h + 1] * jnp.maximum(d, 0.0)
        kpos = k0 + lax.broadcasted_iota(I32, (1, tk), 1)
        key = jnp.where(kpos <= qpos, _order_key(acc), jnp.int32(INT_MIN))
        key_ref[kt] = key
        for c in range(tk // V7X_LANES):
            x = key[:, c * V7X_LANES:(c + 1) * V7X_LANES]
            m2 = jnp.maximum(m2, jnp.minimum(m1, x))
            m1 = jnp.maximum(m1, x)
        return m1, m2

    lowest = jnp.full((tq, V7X_LANES), INT_MIN, I32)
    m1, m2 = lax.fori_loop(0, n_kt, score_tile, (lowest, lowest))
    row_max = jnp.max(m1, axis=-1, keepdims=True)

    def count_ge(thr):
        def body(kt, acc):
            m = jnp.where(key_ref[kt] >= thr, 1, 0).astype(I32)
            return acc + _lane_chunk_sum(m, tk)
        acc = lax.fori_loop(0, n_kt, body, jnp.zeros((tq, V7X_LANES), I32))
        return jnp.sum(acc, axis=-1, keepdims=True)

    two_n = 2.0 * (qpos + 1).astype(F32)
    root_target = jnp.sqrt(jnp.maximum(jnp.log(two_n / (topk - 0.5)), 0.0))

    def tail_coord(cnt):
        c = jnp.maximum(cnt.astype(F32), 0.5)
        return root_target - jnp.sqrt(jnp.maximum(jnp.log(two_n / c), 0.0))

    def any_row(mask):
        return jnp.max(jnp.where(mask, 1, 0)) > 0

    def wide_bracket(st):
        lo, hi, cnt_lo, cnt_hi = st[1:5]
        return any_row((lo + 1 < hi) & (cnt_lo - cnt_hi > SEARCH_COLLECT_MAX))

    def open_bracket(st):
        return any_row(st[1] + 1 < st[2])

    def search_body(st):
        it, lo, hi, cnt_lo, cnt_hi, f_lo, f_hi, side = st
        active = lo + 1 < hi
        vlo = _key_value(lo)
        vhi = _key_value(hi)
        frac = f_lo / (f_lo - f_hi)
        interp = _order_key(vlo + frac * (vhi - vlo))
        bisect = (lo & hi) + ((lo ^ hi) >> 1)
        use_interp = ((lo > KEY_NEG_INF) & (hi <= KEY_POS_INF)
                      & ((it < SEARCH_INTERP_ITERS) | ((it & 1) == 0)))
        mid = jnp.where(use_interp, interp, bisect)
        mid = jnp.minimum(jnp.maximum(mid, lo + 1), hi - 1)
        mid = jnp.where(active, mid, lo)
        cnt = count_ge(mid)
        up = active & (cnt >= topk)
        down = active & (cnt < topk)
        hi = jnp.where(up & (cnt == topk), mid + 1, jnp.where(down, mid, hi))
        lo = jnp.where(up, mid, lo)
        cnt_lo = jnp.where(up, cnt, cnt_lo)
        cnt_hi = jnp.where(down, cnt, cnt_hi)
        f_new = tail_coord(cnt)
        f_lo = jnp.where(up, f_new, jnp.where(down & (side < 0), 0.5 * f_lo, f_lo))
        f_hi = jnp.where(down, f_new, jnp.where(up & (side > 0), 0.5 * f_hi, f_hi))
        side = jnp.where(up, 1, jnp.where(down, -1, side))
        return it + 1, lo, hi, cnt_lo, cnt_hi, f_lo, f_hi, side

    lo0 = jnp.min(m2, axis=-1, keepdims=True)
    cnt0 = count_ge(lo0)
    hi0 = jnp.where(cnt0 == topk, lo0 + 1, row_max + 1)
    zero = jnp.zeros((tq, 1), I32)
    f_lo0 = tail_coord(cnt0)
    f_hi0 = tail_coord(zero)
    st = lax.while_loop(wide_bracket, search_body, (zero, lo0, hi0, cnt0, zero, f_lo0, f_hi0, zero))
    it, lo, hi, cnt_lo, cnt_hi = st[0:5]

    def collect_tile(kt, carry):
        c1, c2, c3 = carry
        x = key_ref[kt]
        x = jnp.where(x >= lo, jnp.where(x < hi, x, jnp.int32(INT_MIN)), jnp.int32(INT_MIN))
        for c in range(tk // V7X_LANES):
            t = x[:, c * V7X_LANES:(c + 1) * V7X_LANES]
            c1, t = jnp.maximum(c1, t), jnp.minimum(c1, t)
            c2, t = jnp.maximum(c2, t), jnp.minimum(c2, t)
            c3 = jnp.maximum(c3, t)
        return c1, c2, c3

    cand = jnp.concatenate(lax.fori_loop(0, n_kt, collect_tile, (lowest, lowest, lowest)), axis=1)
    got = jnp.sum(jnp.where(cand > INT_MIN, 1, 0), axis=-1, keepdims=True)
    complete = got == cnt_lo - cnt_hi

    def small_cond(s):
        return any_row(complete & (s[0] + 1 < s[1]))

    def small_body(s):
        lo, hi, cnt_lo = s
        active = complete & (lo + 1 < hi)
        mid = jnp.where(active, (lo & hi) + ((lo ^ hi) >> 1), lo)
        cnt = cnt_hi + jnp.sum(jnp.where(cand >= mid, 1, 0), axis=-1, keepdims=True)
        up = active & (cnt >= topk)
        down = active & (cnt < topk)
        hi = jnp.where(up & (cnt == topk), mid + 1, jnp.where(down, mid, hi))
        return jnp.where(up, mid, lo), hi, jnp.where(up, cnt, cnt_lo)

    lo, hi, cnt_lo = lax.while_loop(small_cond, small_body, (lo, hi, cnt_lo))
    st = lax.while_loop(open_bracket, search_body, (it, lo, hi, cnt_lo) + tuple(st[4:]))
    thr, cnt_thr = st[1], st[3]

    tie_rows = (cnt_thr > topk) & (thr > INT_MIN)

    @pl.when(jnp.max(jnp.where(tie_rows, 1, 0)) > 0)
    def _():
        need = topk - count_ge(thr + 1)
        a = lax.broadcasted_iota(I32, (tk, tk), 0)
        b = lax.broadcasted_iota(I32, (tk, tk), 1)
        before = jnp.where(a < b, 1.0, 0.0).astype(BF16)

        def body(kt, seen):
            key = key_ref[kt]
            eq = (key == thr) & tie_rows
            eqf = jnp.where(eq, 1.0, 0.0)
            rank = seen + _dot(eqf.astype(BF16), before)
            drop = eq & (rank >= need.astype(F32))
            key_ref[kt] = jnp.where(drop, jnp.int32(INT_MIN), key)
            return seen + jnp.sum(eqf, axis=-1, keepdims=True)

        lax.fori_loop(0, n_kt, body, jnp.zeros((tq, 1), F32))

    thr = jnp.maximum(thr, INT_MIN + 1)

    acc_ref[...] = jnp.zeros(acc_ref.shape, F32)
    for h in range(N_ATT_HEADS):
        qh = q_ref[:, h * ATT_HEAD_DIM:(h + 1) * ATT_HEAD_DIM].astype(F32)
        qn = jnp.sqrt(jnp.sum(qh * qh, axis=-1, keepdims=True))
        m_ref[h] = qn * kmax_ref[h // rep][:, 0:1]
    ones_col = jnp.where(lax.broadcasted_iota(I32, (tk, V7X_LANES), 1) == 0, 1.0, 0.0).astype(BF16)

    def attend_tile(kt, carry):
        k0 = pl.multiple_of(kt * tk, tk)
        keep = jnp.where(key_ref[kt] >= thr, 1.0, 0.0)
        for h in range(N_ATT_HEADS):
            gs = slice((h // rep) * ATT_HEAD_DIM, (h // rep + 1) * ATT_HEAD_DIM)
            qh = q_ref[:, h * ATT_HEAD_DIM:(h + 1) * ATT_HEAD_DIM]
            s_ref[h] = _dot_nt(qh, k_ref[pl.ds(k0, tk), gs])
        for h in range(N_ATT_HEADS):
            gs = slice((h // rep) * ATT_HEAD_DIM, (h // rep + 1) * ATT_HEAD_DIM)
            p = jnp.exp2(s_ref[h] - m_ref[h]) * keep
            v_aug = jnp.concatenate([v_ref[pl.ds(k0, tk), gs], ones_col], axis=1)
            acc_ref[h] += _dot(p.astype(BF16), v_aug)
        return carry

    lax.fori_loop(0, n_kt, attend_tile, 0)
    underflow = jnp.int32(0)
    for h in range(N_ATT_HEADS):
        l = acc_ref[h, :, ATT_HEAD_DIM:ATT_HEAD_DIM + 1]
        underflow = jnp.maximum(underflow, jnp.max(jnp.where(l >= SOFTMAX_MIN_SUM, 0, 1)))
        o_ref[:, h * ATT_HEAD_DIM:(h + 1) * ATT_HEAD_DIM] = (acc_ref[h, :, 0:ATT_HEAD_DIM] / l).astype(o_ref.dtype)

    @pl.when(underflow > 0)
    def _():
        _attend_online(q_ref, k_ref, v_ref, o_ref, key_ref, acc_ref, m_ref, l_ref, thr, n_kt, tk)


def _attend_online(q_ref, k_ref, v_ref, o_ref, key_ref, acc_ref, m_ref, l_ref, thr, n_kt, tk):
    rep = N_ATT_HEADS // N_KV_HEADS
    d = ATT_HEAD_DIM
    m_ref[...] = jnp.full(m_ref.shape, MASKED_LOGIT, F32)
    l_ref[...] = jnp.zeros(l_ref.shape, F32)
    acc_ref[...] = jnp.zeros(acc_ref.shape, F32)

    def online_tile(kt, carry):
        k0 = pl.multiple_of(kt * tk, tk)
        sel = key_ref[kt] >= thr
        for g in range(N_KV_HEADS):
            kg = k_ref[pl.ds(k0, tk), g * d:(g + 1) * d]
            vg = v_ref[pl.ds(k0, tk), g * d:(g + 1) * d]
            for r in range(rep):
                h = g * rep + r
                s = jnp.where(sel, _dot_nt(q_ref[:, h * d:(h + 1) * d], kg), MASKED_LOGIT)
                m_old = m_ref[h]
                m_new = jnp.maximum(m_old, jnp.max(s, axis=-1, keepdims=True))
                alpha = jnp.exp2(m_old - m_new)
                p = jnp.exp2(s - m_new)
                l_ref[h] = alpha * l_ref[h] + jnp.sum(p, axis=-1, keepdims=True)
                acc_ref[h, :, 0:d] = alpha * acc_ref[h, :, 0:d] + _dot(p.astype(BF16), vg)
                m_ref[h] = m_new
        return carry

    lax.fori_loop(0, n_kt, online_tile, 0)
    for h in range(N_ATT_HEADS):
        o_ref[:, h * d:(h + 1) * d] = (acc_ref[h, :, 0:d] / l_ref[h]).astype(o_ref.dtype)


def _dsa(qi_hm, wi, q, ki, k, v_src, v_col, tq, tk, topk):
    s = q.shape[0]
    kw = N_KV_HEADS * ATT_HEAD_DIM
    qw = N_ATT_HEADS * ATT_HEAD_DIM
    whole = lambda shape, idx: pl.BlockSpec(shape, idx, pipeline_mode=pl.Buffered(1))
    kern = functools.partial(_dsa_kernel, tq=tq, tk=tk, topk=topk)
    return pl.pallas_call(
        kern,
        grid=(s // tq,),
        in_specs=[pl.BlockSpec((N_IDX_HEADS, tq, IDX_HEAD_DIM), lambda i: (0, i, 0)),
                  pl.BlockSpec((tq, N_IDX_HEADS), lambda i: (i, 0)),
                  pl.BlockSpec((tq, qw), lambda i: (i, 0)),
                  whole((s, IDX_HEAD_DIM), lambda i: (0, 0)),
                  whole((s, kw), lambda i: (0, 0)),
                  whole((s, kw), lambda i: (0, v_col // kw))],
        out_specs=pl.BlockSpec((tq, qw), lambda i: (i, 0)),
        out_shape=jax.ShapeDtypeStruct((s, qw), BF16),
        scratch_shapes=[pltpu.VMEM((s // tk, tq, tk), I32),
                        pltpu.VMEM((N_ATT_HEADS, tq, 2 * ATT_HEAD_DIM), F32),
                        pltpu.VMEM((N_ATT_HEADS, tq, tk), F32),
                        pltpu.VMEM((N_ATT_HEADS, tq, 1), F32),
                        pltpu.VMEM((N_ATT_HEADS, tq, 1), F32),
                        pltpu.VMEM((N_KV_HEADS, 1, V7X_LANES), F32)],
        compiler_params=_cparams(("arbitrary",), 56),
        name="dsa",
    )(qi_hm, wi, q, ki, k, v_src)


HGRN_BLOCK = 256
HGRN_LEVELS = (32, 64, 128, 256)


def _hgrn_constants():
    tb = HGRN_BLOCK
    t = np.arange(tb)[:, None]
    u = np.arange(tb)[None, :]
    lower = (u <= t).astype(np.float32)
    mats = [lower]
    c0 = HGRN_LEVELS[0]
    ref0 = c0 * (t // c0) + c0 // 2 - 1
    mats.append(lower - (u <= ref0))
    for c in HGRN_LEVELS[1:-1]:
        ref = c * (t // c) + c // 2 - 1
        mats.append(lower - (u <= ref))
    level = np.full((tb, tb), -1.0, np.float32)
    level[(t // c0 == u // c0) & (u <= t)] = 0.0
    for li, c in enumerate(HGRN_LEVELS[1:], start=1):
        sel = (t // c == u // c) & (t % c >= c // 2) & (u % c < c // 2)
        level[sel] = float(li)
    return np.concatenate(mats, axis=0), level


def _hgrn_kernel(q_ref, f_ref, i_ref, g_ref, lb_ref, gn_ref, mats_ref, lvl_ref, o_ref, st_ref):
    @pl.when(pl.program_id(0) == 0)
    def _():
        st_ref[...] = jnp.zeros(st_ref.shape, F32)

    for h in range(N_HGRN_HEADS):
        hs = slice(h * HGRN_DIM, (h + 1) * HGRN_DIM)
        _hgrn_head(q_ref.at[:, hs], f_ref.at[:, hs], i_ref.at[:, hs], g_ref.at[:, hs], lb_ref.at[h],
                   gn_ref, mats_ref, lvl_ref, o_ref.at[:, hs], st_ref.at[h])


def _hgrn_head(q_ref, f_ref, i_ref, g_ref, lb_ref, gn_ref, mats_ref, lvl_ref, o_ref, st_ref):
    tb = HGRN_BLOCK
    d = HGRN_DIM
    q = q_ref[...]
    qf = q * _sigmoid(q)
    lb = lb_ref[...]
    f = lb + (1.0 - lb) * _sigmoid(f_ref[...])
    logf = jnp.log(f)
    kf = 1.0 - f
    v = i_ref[...].astype(BF16)

    g1 = logf.astype(BF16)
    r1 = logf - g1.astype(F32)
    g2 = r1.astype(BF16)
    g3 = (r1 - g2.astype(F32)).astype(BF16)
    sums = _dot(mats_ref[...], jnp.concatenate([g1, g2, g3], axis=1))
    sums = sums[:, 0:d] + sums[:, d:2 * d] + sums[:, 2 * d:3 * d]
    n_mid = len(HGRN_LEVELS) - 1
    cum = sums[0:tb]
    rel = [sums[(l + 1) * tb:(l + 2) * tb] for l in range(n_mid)]
    rel.append(cum - cum[tb // 2 - 1:tb // 2, :])

    lvl = lvl_ref[...]
    amat = jnp.zeros((tb, tb), F32)
    for l, e in enumerate(rel):
        cap = EXP_CLAMP if l == 0 else 0.0
        qs = (qf * jnp.exp(jnp.minimum(e, cap))).astype(BF16)
        ks = (kf * jnp.exp(jnp.minimum(-e, cap))).astype(BF16)
        amat = jnp.where(lvl == float(l), _dot_nt(qs, ks), amat)

    st = st_ref[...]
    o = _dot(amat.astype(BF16), v) + _dot_nt((qf * jnp.exp(cum)).astype(BF16), st.astype(BF16))
    last = cum[tb - 1:tb, :]
    ks_state = (kf * jnp.exp(last - cum)).astype(BF16)
    st_ref[...] = st * jnp.exp(last) + _dot_tn(v, ks_state)

    ms = jnp.mean(o * o, axis=-1, keepdims=True)
    y = o * lax.rsqrt(ms + NORM_EPS) * gn_ref[...]
    gate = g_ref[...].astype(F32)
    o_ref[...] = (y * (gate * _sigmoid(gate))).astype(o_ref.dtype)


def _hgrn(proj_f, proj_b, lb, gn, cols_f, cols_b):
    s = proj_f.shape[0]
    tb = HGRN_BLOCK
    d = HGRN_DIM
    mats_np, lvl_np = _hgrn_constants()
    mats = jnp.asarray(mats_np, BF16)
    lvl = jnp.asarray(lvl_np, F32)
    w = N_HGRN_HEADS * d
    col = lambda off: (lambda j: (j, off // w))
    const = lambda shape: pl.BlockSpec(shape, lambda j: (0,) * len(shape))
    return pl.pallas_call(
        _hgrn_kernel,
        grid=(s // tb,),
        in_specs=[pl.BlockSpec((tb, w), col(cols_f["hq"])),
                  pl.BlockSpec((tb, w), col(cols_f["hf"])),
                  pl.BlockSpec((tb, w), col(cols_f["hi"])),
                  pl.BlockSpec((tb, w), col(cols_b["hg"])),
                  const((N_HGRN_HEADS, 1, d)),
                  const((1, d)),
                  const(mats_np.shape),
                  const(lvl_np.shape)],
        out_specs=pl.BlockSpec((tb, w), lambda j: (j, 0)),
        out_shape=jax.ShapeDtypeStruct((s, w), BF16),
        scratch_shapes=[pltpu.VMEM((N_HGRN_HEADS, d, d), F32)],
        compiler_params=_cparams(("arbitrary",), 32),
        name="hgrn",
    )(proj_f, proj_f, proj_f, proj_b, lb.reshape(N_HGRN_HEADS, 1, d), gn.reshape(1, d), mats, lvl)


def _merge_kernel(att_ref, rec_ref, ga_ref, gr_ref, wpa_ref, wph_ref, o_ref):
    a = _dot(att_ref[...], wpa_ref[...])
    r = _dot(rec_ref[...], wph_ref[...])
    merged = _sigmoid(ga_ref[...].astype(F32)) * a + _sigmoid(gr_ref[...].astype(F32)) * r
    o_ref[...] = merged.astype(o_ref.dtype)


def _merge(att, rec, proj_b, gates_col, wpa, wph, tm):
    s, aw = att.shape
    rw = rec.shape[1]
    d = wpa.shape[1]
    const = lambda shape: pl.BlockSpec(shape, lambda i: (0,) * len(shape), pipeline_mode=pl.Buffered(1))
    row = lambda w: pl.BlockSpec((tm, w), lambda i: (i, 0))
    gcol = gates_col // d
    return pl.pallas_call(
        _merge_kernel,
        grid=(s // tm,),
        in_specs=[row(aw), row(rw),
                  pl.BlockSpec((tm, d), lambda i: (i, gcol)),
                  pl.BlockSpec((tm, d), lambda i: (i, gcol + 1)),
                  const((aw, d)), const((rw, d))],
        out_specs=row(d),
        out_shape=jax.ShapeDtypeStruct((s, d), BF16),
        compiler_params=_cparams(("parallel",), 48),
        name="merge",
    )(att, rec, proj_b, proj_b, wpa, wph)


def _post_kernel(mg_ref, x_ref, wo_ref, gt_ref, g2_ref, sc_ref, sh_ref, wr_ref, br_ref,
                 x1_ref, h2_ref, te_ref, tg_ref, sel_ref):
    mix = _dot(mg_ref[...], wo_ref[...])
    x1 = x_ref[...] + gt_ref[...] * mix
    x1_ref[...] = x1
    ms = jnp.mean(x1 * x1, axis=-1, keepdims=True)
    h2 = x1 * lax.rsqrt(ms + NORM_EPS) * g2_ref[...]
    h2 = h2 * (1.0 + sc_ref[...]) + sh_ref[...]
    h2_ref[...] = _pack_rows(h2)
    ne = br_ref.shape[-1]
    h_hi = h2.astype(BF16)
    h_lo = (h2 - h_hi.astype(F32)).astype(BF16)
    t = _dot(h_hi, wr_ref[...])
    logits = t[:, 0:ne] + t[:, ne:2 * ne] + _dot(h_lo, wr_ref[:, 0:ne]) + br_ref[...]
    lane = lax.broadcasted_iota(I32, logits.shape, 1)
    work = logits
    vals, idxs = [], []
    sel = jnp.zeros(logits.shape, F32)
    for _ in range(MOE_TOPK):
        m = jnp.max(work, axis=-1, keepdims=True)
        idx = jnp.min(jnp.where(work == m, lane, ne), axis=-1, keepdims=True)
        hit = lane == idx
        vals.append(m)
        idxs.append(idx)
        sel = jnp.where(hit, 1.0, sel)
        work = jnp.where(hit, -jnp.inf, work)
    es = [jnp.exp(vk - vals[0]) for vk in vals]
    denom = es[0]
    for e in es[1:]:
        denom = denom + e
    lane4 = lax.broadcasted_iota(I32, te_ref.shape, 1)
    te = jnp.zeros(te_ref.shape, I32)
    tg = jnp.zeros(tg_ref.shape, F32)
    for kk in range(MOE_TOPK):
        te = jnp.where(lane4 == kk, idxs[kk], te)
        tg = jnp.where(lane4 == kk, es[kk] / denom, tg)
    te_ref[...] = te
    tg_ref[...] = tg
    sel_ref[...] = sel.astype(sel_ref.dtype)


def _post_mixer(merged, x, wo, gt1, g2, sc2, sh2, wr, br, tm):
    s, d = x.shape
    ne = br.shape[1]
    wr_hi = wr.astype(BF16)
    wr = jnp.concatenate([wr_hi, (wr - wr_hi.astype(F32)).astype(BF16)], axis=1)
    const = lambda shape: pl.BlockSpec(shape, lambda i: (0,) * len(shape), pipeline_mode=pl.Buffered(1))
    row = lambda w: pl.BlockSpec((tm, w), lambda i: (i, 0))
    return pl.pallas_call(
        _post_kernel,
        grid=(s // tm,),
        in_specs=[row(d), row(d), const((d, d)),
                  const((1, d)), const((1, d)), const((1, d)), const((1, d)),
                  const((d, 2 * ne)), const((1, ne))],
        out_specs=[row(d), row(d // 2), row(MOE_TOPK), row(MOE_TOPK), row(ne)],
        out_shape=[jax.ShapeDtypeStruct((s, d), F32),
                   jax.ShapeDtypeStruct((s, d // 2), I32),
                   jax.ShapeDtypeStruct((s, MOE_TOPK), I32),
                   jax.ShapeDtypeStruct((s, MOE_TOPK), F32),
                   jax.ShapeDtypeStruct((s, ne), BF16)],
        compiler_params=_cparams(("parallel",), 56),
        name="post_mixer",
    )(merged, x, wo, gt1, g2, sc2, sh2, wr, br)


def _rank_kernel(sel_ref, rank_ref, cnt_ref, carry_ref):
    @pl.when(pl.program_id(0) == 0)
    def _():
        carry_ref[...] = jnp.zeros(carry_ref.shape, F32)

    sel = sel_ref[...]
    tt = sel.shape[0]
    a = lax.broadcasted_iota(I32, (tt, tt), 0)
    b = lax.broadcasted_iota(I32, (tt, tt), 1)
    before = jnp.where(b < a, 1.0, 0.0).astype(BF16)
    carry = carry_ref[...]
    rank_ref[...] = _dot(before, sel) + carry
    carry = carry + jnp.sum(sel.astype(F32), axis=0, keepdims=True)
    carry_ref[...] = carry
    cnt_ref[...] = carry


def _expert_ranks(sel, tt):
    s, ne = sel.shape
    return pl.pallas_call(
        _rank_kernel,
        grid=(s // tt,),
        in_specs=[pl.BlockSpec((tt, ne), lambda i: (i, 0))],
        out_specs=[pl.BlockSpec((tt, ne), lambda i: (i, 0)),
                   pl.BlockSpec((1, ne), lambda i: (0, 0))],
        out_shape=[jax.ShapeDtypeStruct((s, ne), F32),
                   jax.ShapeDtypeStruct((1, ne), F32)],
        scratch_shapes=[pltpu.VMEM((1, ne), F32)],
        compiler_params=_cparams(("arbitrary",), 16),
        name="expert_ranks",
    )(sel)


DMA_ISSUE_UNROLL = 4


def _row_copy(src_ref, src_row, dst_ref, dst_row, sem):
    return pltpu.make_async_copy(src_ref.at[pl.ds(src_row, 1)], dst_ref.at[pl.ds(dst_row, 1)], sem)


def _scatter_kernel(dest_ref, h_ref, xb_in_ref, xb_ref, sem, *, tt):
    del xb_in_ref
    base = pl.program_id(0) * tt * MOE_TOPK

    def issue(t, c):
        for kk in range(MOE_TOPK):
            _row_copy(h_ref, t, xb_ref, dest_ref[base + t * MOE_TOPK + kk], sem).start()
        return c

    lax.fori_loop(0, tt, issue, 0, unroll=DMA_ISSUE_UNROLL)

    def drain(t, c):
        for kk in range(MOE_TOPK):
            _row_copy(h_ref, 0, xb_ref, 0, sem).wait()
        return c

    lax.fori_loop(0, tt, drain, 0)


def _scatter_rows(dest_flat, h2, n_rows, tt):
    s, d = h2.shape
    xb0 = jnp.zeros((n_rows, d), h2.dtype)
    grid_spec = pltpu.PrefetchScalarGridSpec(
        num_scalar_prefetch=1,
        grid=(s // tt,),
        in_specs=[pl.BlockSpec((tt, d), lambda i, dest: (i, 0)),
                  pl.BlockSpec(memory_space=pl.ANY)],
        out_specs=pl.BlockSpec(memory_space=pl.ANY),
        scratch_shapes=[pltpu.SemaphoreType.DMA(())],
    )
    return pl.pallas_call(
        functools.partial(_scatter_kernel, tt=tt),
        grid_spec=grid_spec,
        out_shape=jax.ShapeDtypeStruct((n_rows, d), h2.dtype),
        input_output_aliases={2: 0},
        compiler_params=_cparams(("arbitrary",), 32),
        name="moe_scatter",
    )(dest_flat, h2, xb0)


ITEM_UNUSED, ITEM_USED, ITEM_FIRST = 0, 1, 3


def _unpack_rows(words):
    lo = lax.bitcast_convert_type(words << 16, F32)
    hi = lax.bitcast_convert_type(words & jnp.int32(-65536), F32)
    return jnp.concatenate([lo.astype(BF16), hi.astype(BF16)], axis=1)


def _pack_rows(x):
    half = x.shape[1] // 2
    lo = lax.bitcast_convert_type(x[:, :half].astype(BF16).astype(F32), I32)
    hi = lax.bitcast_convert_type(x[:, half:].astype(BF16).astype(F32), I32)
    return lax.shift_right_logical(lo, jnp.int32(16)) | (hi & jnp.int32(-65536))


N_WEIGHT_SLABS = 4
N_ITEM_TABLES = 6 + 2 * (N_WEIGHT_SLABS - 1)


def _expert_up_kernel(*refs):
    flag_ref, val_ref = refs[4], refs[5]
    x_ref = refs[N_ITEM_TABLES]
    w1g_refs = refs[N_ITEM_TABLES + 1:N_ITEM_TABLES + 1 + N_WEIGHT_SLABS]
    w1l_refs = refs[N_ITEM_TABLES + 1 + N_WEIGHT_SLABS:N_ITEM_TABLES + 1 + 2 * N_WEIGHT_SLABS]
    b1g_ref, b1l_ref, a_ref, wg_ref, wl_ref = refs[N_ITEM_TABLES + 1 + 2 * N_WEIGHT_SLABS:]
    i = pl.program_id(0)
    flag = flag_ref[i]

    @pl.when(flag == ITEM_FIRST)
    def _():
        rows = wg_ref.shape[0] // N_WEIGHT_SLABS
        for k in range(N_WEIGHT_SLABS):
            wg_ref[k * rows:(k + 1) * rows, :] = w1g_refs[k][0].astype(BF16)
            wl_ref[k * rows:(k + 1) * rows, :] = w1l_refs[k][0].astype(BF16)

    @pl.when(flag != ITEM_UNUSED)
    def _():
        rows = lax.broadcasted_iota(I32, (x_ref.shape[0], 1), 0)
        x = _unpack_rows(jnp.where(rows < val_ref[i], x_ref[...], 0))
        ug = _dot(x, wg_ref[...]) + b1g_ref[0]
        ul = _dot(x, wl_ref[...]) + b1l_ref[0]
        ug = jnp.minimum(ug, SWIGLU_LIMIT)
        ul = jnp.clip(ul, -SWIGLU_LIMIT, SWIGLU_LIMIT)
        a_ref[...] = (ug * _sigmoid(SWIGLU_ALPHA * ug) * (ul + 1.0)).astype(a_ref.dtype)

    @pl.when(flag == ITEM_UNUSED)
    def _():
        a_ref[...] = jnp.zeros(a_ref.shape, a_ref.dtype)


def _expert_down_kernel(*refs):
    flag_ref = refs[4]
    a_ref = refs[N_ITEM_TABLES]
    w2_refs = refs[N_ITEM_TABLES + 1:N_ITEM_TABLES + 1 + N_WEIGHT_SLABS]
    b2_ref, y_ref, w_ref = refs[N_ITEM_TABLES + 1 + N_WEIGHT_SLABS:]
    i = pl.program_id(0)
    flag = flag_ref[i]

    @pl.when(flag == ITEM_FIRST)
    def _():
        rows = w_ref.shape[0] // N_WEIGHT_SLABS
        for k in range(N_WEIGHT_SLABS):
            w_ref[k * rows:(k + 1) * rows, :] = w2_refs[k][0].astype(BF16)

    @pl.when(flag != ITEM_UNUSED)
    def _():
        y_ref[...] = _dot(a_ref[...], w_ref[...]) + b2_ref[0]

    @pl.when(flag == ITEM_UNUSED)
    def _():
        y_ref[...] = jnp.zeros(y_ref.shape, y_ref.dtype)


def _expert_items(sb_expert, sb_valid, n_used, pstart, pcnt, n_sb, rb, nch):
    idx = jnp.arange(n_sb * nch, dtype=I32)
    nu = n_used[0]
    used = idx < nu * nch
    e = sb_expert[jnp.minimum(idx // nch, nu - 1)]
    first_sb = pstart[e] // rb
    n_blk = jnp.maximum(pcnt[e] // rb, 1)
    local = idx - nch * first_sb
    n_free = jnp.maximum(n_sb - nu, 1)
    spare = idx - nch * nu
    item_sb = jnp.where(used, first_sb + local % n_blk, nu + spare % n_free)
    chunk = jnp.where(used, local // n_blk, spare // n_free)
    item_wc = jnp.where(used, chunk, nch - 1)
    flag = jnp.where(used, jnp.where(local % n_blk == 0, ITEM_FIRST, ITEM_USED), ITEM_UNUSED)
    left = jnp.where(used, n_blk - 1 - local % n_blk, 0)
    nxt = jnp.minimum(idx + left + 1, n_sb * nch - 1)
    e_next, wc_next = e[nxt], item_wc[nxt]
    early = []
    for k in range(1, N_WEIGHT_SLABS):
        take = used & (local % n_blk > 0) & (left < k)
        early += [jnp.where(take, e_next, e).astype(I32), jnp.where(take, wc_next, item_wc).astype(I32)]
    return (item_sb.astype(I32), item_wc.astype(I32), chunk.astype(I32), e.astype(I32),
            flag.astype(I32), sb_valid[item_sb].astype(I32), *early)


def _experts(items, xb, w1, b1, w2, b2, rb, th):
    n_rows = xb.shape[0]
    ne, d, two_de = w1.shape
    de = two_de // 2
    nch = de // th
    n_items = items[0].shape[0]
    b1r = b1.reshape(ne, 1, two_de)
    ns = N_WEIGHT_SLABS
    slab_e = lambda k: 3 if k == 0 else 6 + 2 * (k - 1)
    slab_c = lambda k: 1 if k == 0 else 7 + 2 * (k - 1)

    def slab_spec(rows, cols, k, col0):
        return pl.BlockSpec((1, rows // ns, cols),
                            lambda i, *t: (t[slab_e(k)][i], k, col0 + t[slab_c(k)][i]))

    up_spec = pltpu.PrefetchScalarGridSpec(
        num_scalar_prefetch=N_ITEM_TABLES,
        grid=(n_items,),
        in_specs=([pl.BlockSpec((rb, d // 2), lambda i, *t: (t[0][i], 0))]
                  + [slab_spec(d, th, k, 0) for k in range(ns)]
                  + [slab_spec(d, th, k, nch) for k in range(ns)]
                  + [pl.BlockSpec((1, 1, th), lambda i, *t: (t[3][i], 0, t[1][i])),
                     pl.BlockSpec((1, 1, th), lambda i, *t: (t[3][i], 0, nch + t[1][i]))]),
        out_specs=pl.BlockSpec((rb, th), lambda i, *t: (t[0][i], t[2][i])),
        scratch_shapes=[pltpu.VMEM((d, th), BF16), pltpu.VMEM((d, th), BF16)],
    )
    act = pl.pallas_call(
        _expert_up_kernel,
        grid_spec=up_spec,
        out_shape=jax.ShapeDtypeStruct((n_rows, de), BF16),
        compiler_params=_cparams(("arbitrary",), 58),
        name="moe_expert_up",
    )(*items, xb, *([w1] * (2 * ns)), b1r, b1r)
    tn = d // nch
    down_spec = pltpu.PrefetchScalarGridSpec(
        num_scalar_prefetch=N_ITEM_TABLES,
        grid=(n_items,),
        in_specs=([pl.BlockSpec((rb, de), lambda i, *t: (t[0][i], 0))]
                  + [slab_spec(de, tn, k, 0) for k in range(ns)]
                  + [pl.BlockSpec((1, 1, tn), lambda i, *t: (t[3][i], 0, t[1][i]))]),
        out_specs=pl.BlockSpec((rb, tn), lambda i, *t: (t[0][i], t[2][i])),
        scratch_shapes=[pltpu.VMEM((de, tn), BF16)],
    )
    return pl.pallas_call(
        _expert_down_kernel,
        grid_spec=down_spec,
        out_shape=jax.ShapeDtypeStruct((n_rows, d), F32),
        compiler_params=_cparams(("arbitrary",), 48),
        name="moe_expert_down",
    )(*items, act, *([w2] * ns), b2.reshape(ne, 1, d))


def _combine_kernel(dest_ref, yb_ref, x1_ref, tg_ref, gt_ref, gn_ref, o_ref, buf_ref, sem, *, tt):
    i = pl.program_id(0)
    slot = i % 2

    def gather(step, into):
        base = step * tt * MOE_TOPK

        def issue(t, c):
            for kk in range(MOE_TOPK):
                _row_copy(yb_ref, dest_ref[base + t * MOE_TOPK + kk], buf_ref.at[into, kk], t,
                          sem.at[into]).start()
            return c

        lax.fori_loop(0, tt, issue, 0, unroll=DMA_ISSUE_UNROLL)

    @pl.when(i == 0)
    def _():
        gather(0, 0)

    @pl.when(i + 1 < pl.num_programs(0))
    def _():
        gather(i + 1, 1 - slot)

    def drain(t, c):
        for kk in range(MOE_TOPK):
            _row_copy(yb_ref, 0, buf_ref.at[slot, kk], 0, sem.at[slot]).wait()
        return c

    lax.fori_loop(0, tt, drain, 0)

    tg = tg_ref[...]
    moe = tg[:, 0:1] * buf_ref[slot, 0]
    for kk in range(1, MOE_TOPK):
        moe = moe + tg[:, kk:kk + 1] * buf_ref[slot, kk]
    x2 = x1_ref[...] + gt_ref[...] * moe
    ms = jnp.mean(x2 * x2, axis=-1, keepdims=True)
    o_ref[...] = x2 * lax.rsqrt(ms + NORM_EPS) * gn_ref[...]


def _combine(dest_flat, yb, x1, tg, gt2, gn, tt):
    s, d = x1.shape
    grid_spec = pltpu.PrefetchScalarGridSpec(
        num_scalar_prefetch=1,
        grid=(s // tt,),
        in_specs=[pl.BlockSpec(memory_space=pl.ANY),
                  pl.BlockSpec((tt, d), lambda i, dest: (i, 0)),
                  pl.BlockSpec((tt, MOE_TOPK), lambda i, dest: (i, 0)),
                  pl.BlockSpec((1, d), lambda i, dest: (0, 0)),
                  pl.BlockSpec((1, d), lambda i, dest: (0, 0))],
        out_specs=pl.BlockSpec((tt, d), lambda i, dest: (i, 0)),
        scratch_shapes=[pltpu.VMEM((2, MOE_TOPK, tt, d), F32),
                        pltpu.SemaphoreType.DMA((2,))],
    )
    return pl.pallas_call(
        functools.partial(_combine_kernel, tt=tt),
        grid_spec=grid_spec,
        out_shape=jax.ShapeDtypeStruct((s, d), F32),
        compiler_params=_cparams(("arbitrary",), 40),
        name="moe_combine",
    )(dest_flat, yb, x1, tg, gt2, gn)


MOE_ROW_BLOCK = 512
MOE_HIDDEN_TILE = 1024


def _pad_cols(w, n):
    return jnp.pad(w, ((0, 0), (0, n - w.shape[1])))


def kernel(x, c, positions, w_ada, b_ada, norm1_g, w_in, idx_knorm_g, idx_knorm_b, hgrn_lb_logits,
           hgrn_norm_g, w_proj_att, w_proj_hgrn, w_out, norm2_g, w_router, b_router, w1, b1, w2, b2,
           final_norm_g):
    bsz, s, d = x.shape
    assert bsz == 1 and w_ada.shape[0] == 1
    xs = x.reshape(s, d)
    pos = positions.reshape(s)

    mod = _adaln(c, w_ada[0], b_ada[0])
    sh1, sc1, gt1, sh2, sc2, gt2 = [mod[:, i * d:(i + 1) * d] for i in range(6)]

    widths = dict(q=N_ATT_HEADS * ATT_HEAD_DIM, k=N_KV_HEADS * ATT_HEAD_DIM, v=N_KV_HEADS * ATT_HEAD_DIM,
                  qi=N_IDX_HEADS * IDX_HEAD_DIM, ki=IDX_HEAD_DIM, wi=N_IDX_HEADS,
                  hq=N_HGRN_HEADS * HGRN_DIM, hf=N_HGRN_HEADS * HGRN_DIM, hi=N_HGRN_HEADS * HGRN_DIM,
                  hg=N_HGRN_HEADS * HGRN_DIM, gates=2 * d)
    src, off = {}, 0
    for name in ("q", "k", "v", "qi", "ki", "wi", "hq", "hf", "hi", "hg", "gates"):
        src[name] = (off, widths[name])
        off += widths[name]
    w0 = w_in[0]
    take = lambda name: lax.slice_in_dim(w0, src[name][0], src[name][0] + src[name][1], axis=1)
    kiwi = _pad_cols(jnp.concatenate([take("ki"), take("wi")], axis=1), V7X_LANES)
    f_names = ("q", "qi", "hq", "hf", "hi", "k")
    f_parts = [take(n) for n in f_names] + [kiwi]
    cols_f, o = {}, 0
    for n, p in zip(f_names + ("kiwi",), f_parts):
        cols_f[n] = o
        o += p.shape[1]
    tn_f = 2816
    w_f = _pad_cols(jnp.concatenate(f_parts, axis=1), -(-o // tn_f) * tn_f).astype(BF16)
    b_names = ("gates", "hg", "v")
    cols_b, o = {}, 0
    for n in b_names:
        cols_b[n] = o
        o += widths[n]
    w_b = jnp.concatenate([take(n) for n in b_names], axis=1).astype(BF16)

    g1 = norm1_g[0].reshape(1, d)
    proj_f, h1 = _norm_matmul(xs, g1, sc1, sh1, w_f, F32, 512, tn_f, "in_proj_f32")
    proj_b = _matmul(h1, w_b, BF16, 1024, 1792, "in_proj_bf16")

    q_r, k_r, qi_hm, ki_r, wi_r = _dsa_prep(proj_f, pos, idx_knorm_g[0], idx_knorm_b[0], cols_f, 512)
    topk = min(IDX_TOPK, s // 4)
    att = _dsa(qi_hm, wi_r, q_r, ki_r, k_r, proj_b, cols_b["v"], 128, 1024, topk)

    lb_all = jnp.cumsum(jax.nn.softmax(hgrn_lb_logits.astype(F32), axis=0), axis=0)
    rec = _hgrn(proj_f, proj_b, lb_all[0], hgrn_norm_g[0], cols_f, cols_b)

    merged = _merge(att, rec, proj_b, cols_b["gates"],
                    w_proj_att[0].astype(BF16), w_proj_hgrn[0].astype(BF16), 512)
    x1, h2, top_e, top_g, sel = _post_mixer(
        merged, xs, w_out[0].astype(BF16),
        gt1, norm2_g[0].reshape(1, d), sc2, sh2, w_router[0], b_router[0].reshape(1, -1), 512)

    rank, counts = _expert_ranks(sel, 512)
    rb = MOE_ROW_BLOCK
    n_sb = s * MOE_TOPK // rb + N_EXPERTS
    cnt = counts[0].astype(I32)
    pcnt = (cnt + rb - 1) // rb * rb
    pend = jnp.cumsum(pcnt)
    pstart = pend - pcnt
    dest = jnp.take_along_axis(rank.astype(I32) + pstart[None, :], top_e, axis=1).reshape(-1)
    sb_row = jnp.arange(n_sb, dtype=I32) * rb
    sb_expert = jnp.minimum(jnp.sum((pend[None, :] <= sb_row[:, None]).astype(I32), axis=1), N_EXPERTS - 1)
    sb_valid = jnp.clip(cnt[sb_expert] - (sb_row - pstart[sb_expert]), 0, rb).astype(I32)
    n_used = (pend[-1] // rb).astype(I32).reshape(1)

    xb = _scatter_rows(dest, h2, n_sb * rb, 512)
    items = _expert_items(sb_expert, sb_valid, n_used, pstart, pcnt, n_sb, rb, w2.shape[2] // MOE_HIDDEN_TILE)
    yb = _experts(items, xb, w1[0], b1[0], w2[0], b2[0], rb, MOE_HIDDEN_TILE)
    out = _combine(dest, yb, x1, top_g, gt2, final_norm_g.reshape(1, d), 256)
    return out.reshape(bsz, s, d)
```
